```python
import math
import jax
import jax.numpy as jnp
from jax import lax
import numpy as np

D_MODEL = 1024
BATCH = 4
SEQ = 8192
DEPTH = 4

GRID_W = 64
CTX_LEN = 256
N_MOD = 6
EPS = 1e-6

A_HEADS = 4
A_HEAD_DIM = 64
A_V_DIM = 2 * A_HEAD_DIM
A_WIDTH = A_HEADS * A_V_DIM
Q_BLOCK = 128
ROPE_THETA = 10000.0
SUBLN_EPS = 1e-5

CONV_CH = 512
CONV_K = 31

C_HEADS = 4
C_HEAD_DIM = 128
C_WIDTH = C_HEADS * C_HEAD_DIM
SHORT_K = 5
CHUNK = 64

N_BRANCH = 3
BRANCH_W = 512

A_COLS = 3 * A_WIDTH
C_COLS = 4 * C_WIDTH + 4 * C_HEADS
B_COLS = 2 * CONV_CH
GATE_COLS = N_BRANCH * D_MODEL
CTX_COLS = A_COLS + C_COLS
IN_COLS = A_COLS + C_COLS + B_COLS + GATE_COLS

N_EXPERTS = 16
N_GROUPS = 4
EXPERTS_PER_GROUP = N_EXPERTS // N_GROUPS
TOP_K = 2
D_EXPERT = 1024
MOE_BLOCK = 128

kernel_name = "hybrid_diffusion_prefix_trunk"


def rms_norm(x, w, eps=EPS):
    xf = x.astype(jnp.float32)
    y = xf * lax.rsqrt(jnp.mean(xf * xf, axis=-1, keepdims=True) + eps)
    return (y * w.astype(jnp.float32)).astype(x.dtype)


def layer_norm(x, w, b, eps=1e-5):
    xf = x.astype(jnp.float32)
    mu = jnp.mean(xf, axis=-1, keepdims=True)
    var = jnp.mean(jnp.square(xf - mu), axis=-1, keepdims=True)
    y = (xf - mu) * lax.rsqrt(var + eps)
    return (y * w.astype(jnp.float32) + b.astype(jnp.float32)).astype(x.dtype)


def l2_normalize(x, eps=1e-6):
    xf = x.astype(jnp.float32)
    return xf * lax.rsqrt(jnp.sum(xf * xf, axis=-1, keepdims=True) + eps)


def modulate(x, w, shift, scale):
    return rms_norm(x, w) * (1 + scale) + shift


def depthwise_conv(x, w):
    k = w.shape[0]
    return lax.conv_general_dilated(
        x, w[:, None, :].astype(x.dtype), window_strides=(1,), padding=[(k // 2, k // 2)],
        dimension_numbers=("NWC", "WIO", "NWC"), feature_group_count=x.shape[-1])


def axial_rope_tables(n_tokens):
    rows = n_tokens // GRID_W
    r, col = jnp.meshgrid(jnp.arange(rows, dtype=jnp.float32), jnp.arange(GRID_W, dtype=jnp.float32), indexing="ij")
    axis_dim = A_HEAD_DIM // 2
    inv_freq = ROPE_THETA ** (-jnp.arange(0, axis_dim, 2, dtype=jnp.float32) / axis_dim)
    ang_r = r.reshape(-1, 1) * inv_freq
    ang_c = col.reshape(-1, 1) * inv_freq
    shape = (n_tokens, 1, 1, axis_dim // 2)
    return (jnp.cos(ang_r).reshape(shape), jnp.sin(ang_r).reshape(shape),
            jnp.cos(ang_c).reshape(shape), jnp.sin(ang_c).reshape(shape))


def rope_rotate(x, cos, sin):
    x1, x2 = jnp.split(x, 2, axis=-1)
    cos = cos.astype(x.dtype)
    sin = sin.astype(x.dtype)
    return jnp.concatenate([x1 * cos - x2 * sin, x2 * cos + x1 * sin], axis=-1)


def apply_axial_rope(x, tabs):
    cos_r, sin_r, cos_c, sin_c = tabs
    x_row, x_col = jnp.split(x, 2, axis=-1)
    return jnp.concatenate([rope_rotate(x_row, cos_r, sin_r), rope_rotate(x_col, cos_c, sin_c)], axis=-1)


def diff_softmax_attend(q, k, v, lam):
    s = jnp.einsum("bhmqd,bhmkd->bhmqk", q, k, preferred_element_type=jnp.float32) * (A_HEAD_DIM ** -0.5)
    p = jax.nn.softmax(s, axis=-1)
    a = p[:, :, 0] - lam * p[:, :, 1]
    return jnp.einsum("bhqk,bhkd->bhqd", a.astype(v.dtype), v)


def diff_attn_branch(a_lat, a_ctx, rope, lambda_init, need_ctx, qn_w, kn_w, lam_p, subln_w):
    def heads(p):
        b, t, _ = p.shape
        q, k, v = jnp.split(p, 3, axis=-1)
        q = rms_norm(q.reshape(b, t, A_HEADS, 2, A_HEAD_DIM), qn_w)
        k = rms_norm(k.reshape(b, t, A_HEADS, 2, A_HEAD_DIM), kn_w)
        return q, k, v.reshape(b, t, A_HEADS, A_V_DIM)

    def to_bhm(t):
        return t.transpose(0, 2, 3, 1, 4)

    def finish(o):
        return (rms_norm(o, subln_w, SUBLN_EPS) * (1.0 - lambda_init)).reshape(o.shape[0], o.shape[1], A_WIDTH)

    lq, lk, lv = heads(a_lat)
    cq, ck, cv = heads(a_ctx)
    lq = apply_axial_rope(lq, rope)
    lk = apply_axial_rope(lk, rope)
    lp = lam_p.astype(jnp.float32)
    lam = jnp.exp(jnp.sum(lp[0] * lp[1])) - jnp.exp(jnp.sum(lp[2] * lp[3])) + lambda_init

    b, s = a_lat.shape[:2]
    nb = s // Q_BLOCK
    k_all = jnp.concatenate([to_bhm(ck), to_bhm(lk)], axis=3)
    v_all = jnp.concatenate([cv, lv], axis=1).transpose(0, 2, 1, 3)
    q_blocks = to_bhm(lq).reshape(b, A_HEADS, 2, nb, Q_BLOCK, A_HEAD_DIM).transpose(3, 0, 1, 2, 4, 5)
    o = lax.map(lambda qb: diff_softmax_attend(qb, k_all, v_all, lam), q_blocks)
    y_lat = finish(o.transpose(1, 0, 3, 2, 4).reshape(b, s, A_HEADS, A_V_DIM))
    if not need_ctx:
        return y_lat, None
    oc = diff_softmax_attend(to_bhm(cq), to_bhm(ck), cv.transpose(0, 2, 1, 3), lam)
    return y_lat, finish(oc.transpose(0, 2, 1, 3))


def conformer_conv(u, dw_w, dw_b, ln_w, ln_b):
    a, g = jnp.split(u, 2, axis=-1)
    y = a * jax.nn.sigmoid(g)
    y = depthwise_conv(y, dw_w) + dw_b.astype(y.dtype)
    return jax.nn.silu(layer_norm(y, ln_w, ln_b))


def gated_delta_chunked(q, k, v, g, beta, state, with_output):
    f32 = jnp.float32
    b, t, h, _ = q.shape
    n = t // CHUNK

    def chunks(x):
        return x.astype(f32).reshape(b, n, CHUNK, h, -1).transpose(1, 0, 3, 2, 4)

    q, k, v = chunks(q), chunks(k), chunks(v)
    g = jnp.cumsum(chunks(g[..., None])[..., 0], axis=-1)
    beta = chunks(beta[..., None])[..., 0]
    idx = jnp.arange(CHUNK)
    incl = idx[:, None] >= idx[None, :]
    strict = idx[:, None] > idx[None, :]
    diff = g[..., :, None] - g[..., None, :]
    decay = jnp.where(incl, jnp.exp(jnp.where(incl, diff, 0.0)), 0.0)
    kb = k * beta[..., None]
    lower = jnp.where(strict, jnp.einsum("nbhid,nbhjd->nbhij", kb, k) * decay, 0.0)
    eye = jnp.eye(CHUNK, dtype=f32)
    t_inv = lax.linalg.triangular_solve(lower + eye, jnp.broadcast_to(eye, lower.shape),
                                        left_side=True, lower=True, unit_diagonal=True)
    u = t_inv @ (v * beta[..., None])
    w = t_inv @ (kb * jnp.exp(g)[..., None])
    g_last = g[..., -1]
    k_dec = k * jnp.exp(g_last[..., None] - g)[..., None]

    def advance(s, u_i, w_i, kd_i, gl_i):
        v_new = u_i - w_i @ s
        s_next = s * jnp.exp(gl_i)[..., None, None] + jnp.swapaxes(kd_i, -1, -2) @ v_new
        return s_next, v_new

    if not with_output:
        def step_state(s, xs):
            return advance(s, *xs)[0], None
        state, _ = lax.scan(step_state, state, (u, w, k_dec, g_last))
        return None, state

    q_dec = q * jnp.exp(g)[..., None]
    intra = jnp.where(incl, jnp.einsum("nbhid,nbhjd->nbhij", q, k) * decay, 0.0)

    def step(s, xs):
        u_i, w_i, kd_i, gl_i, qd_i, a_i = xs
        s_next, v_new = advance(s, u_i, w_i, kd_i, gl_i)
        return s_next, qd_i @ s + a_i @ v_new

    state, o = lax.scan(step, state, (u, w, k_dec, g_last, q_dec, intra))
    return o.transpose(1, 0, 3, 2, 4).reshape(b, t, h, v.shape[-1]), state


def gdn_prepare(cols, conv_w, a_log, dt_bias):
    f32 = jnp.float32
    b, t, _ = cols.shape
    qkv = jax.nn.silu(depthwise_conv(cols[..., :3 * C_WIDTH], conv_w))
    q, k, v = jnp.split(qkv, 3, axis=-1)
    q = l2_normalize(q.reshape(b, t, C_HEADS, C_HEAD_DIM)) * (C_HEAD_DIM ** -0.5)
    k = l2_normalize(k.reshape(b, t, C_HEADS, C_HEAD_DIM))
    v = v.reshape(b, t, C_HEADS, C_HEAD_DIM).astype(f32)
    o = 4 * C_WIDTH
    beta = jax.nn.sigmoid(cols[..., o:o + 2 * C_HEADS].astype(f32)).reshape(b, t, 2, C_HEADS)
    a = cols[..., o + 2 * C_HEADS:].astype(f32).reshape(b, t, 2, C_HEADS)
    g = -jnp.exp(a_log.astype(f32)) * jax.nn.softplus(a + dt_bias.astype(f32))
    return q, k, v, g, beta


def gdn_gated_out(o, z, norm_w):
    b, t = z.shape[:2]
    y = rms_norm(o, norm_w) * jax.nn.silu(z.reshape(b, t, C_HEADS, C_HEAD_DIM).astype(jnp.float32))
    return y.reshape(b, t, C_WIDTH).astype(z.dtype)


def orient(x, direction):
    return jnp.flip(x, axis=1) if direction == 1 else x


def gdn_branch(c_lat, c_ctx, need_ctx, conv_w, a_log, dt_bias, norm_w):
    lat = gdn_prepare(c_lat, conv_w, a_log, dt_bias)
    ctx = gdn_prepare(c_ctx, conv_w, a_log, dt_bias)
    s0 = jnp.zeros((c_lat.shape[0], C_HEADS, C_HEAD_DIM, C_HEAD_DIM), jnp.float32)
    o_lat, o_ctx = 0.0, 0.0
    for d in range(2):
        def seqs(pack):
            q, k, v, g, beta = pack
            return [orient(x, d) for x in (q, k, v, g[:, :, d], beta[:, :, d])]
        oc, s_ctx = gated_delta_chunked(*seqs(ctx), s0, need_ctx)
        ol, _ = gated_delta_chunked(*seqs(lat), s_ctx, True)
        o_lat = o_lat + orient(ol, d)
        if need_ctx:
            o_ctx = o_ctx + orient(oc, d)
    y_lat = gdn_gated_out(o_lat, c_lat[..., 3 * C_WIDTH:4 * C_WIDTH], norm_w)
    if not need_ctx:
        return y_lat, None
    return y_lat, gdn_gated_out(o_ctx, c_ctx[..., 3 * C_WIDTH:4 * C_WIDTH], norm_w)


def merge_branches(ys, gate_logits, w_branch, w_out):
    d = w_out.shape[0]
    gates = jax.nn.sigmoid(gate_logits)
    m = 0.0
    for i, y in enumerate(ys):
        m = m + gates[..., i * d:(i + 1) * d] * (y.astype(w_branch.dtype) @ w_branch[i])
    return m @ w_out


def route(h, router_w, router_bias):
    scores = jax.nn.sigmoid((h @ router_w).astype(jnp.float32))
    grp = (scores + router_bias.astype(jnp.float32)).reshape(-1, N_GROUPS, EXPERTS_PER_GROUP)
    best = jnp.argmax(lax.top_k(grp, TOP_K)[0].sum(-1), axis=-1)
    in_grp = jnp.take_along_axis(grp, best[:, None, None], axis=1)[:, 0]
    _, local = lax.top_k(in_grp, TOP_K)
    expert = best[:, None] * EXPERTS_PER_GROUP + local
    w = jnp.take_along_axis(scores, expert, axis=-1)
    return expert, w / jnp.sum(w, axis=-1, keepdims=True)


def moe_ffn(h, router_w, router_bias, w_gate, w_up, w_down):
    n, d = h.shape
    expert, gate = route(h, router_w, router_bias)
    n_assign = n * TOP_K
    flat_e = expert.reshape(-1)
    order = jnp.argsort(flat_e)
    sorted_e = flat_e[order]
    tok = order // TOP_K
    counts = jnp.zeros((N_EXPERTS,), jnp.int32).at[flat_e].add(1)
    padded = (counts + MOE_BLOCK - 1) // MOE_BLOCK * MOE_BLOCK
    pad_end = jnp.cumsum(padded)
    seg_start = jnp.cumsum(counts) - counts
    dest = (pad_end - padded)[sorted_e] + jnp.arange(n_assign) - seg_start[sorted_e]
    n_blocks = -(-n_assign // MOE_BLOCK) + N_EXPERTS
    rows = jnp.zeros((n_blocks * MOE_BLOCK, d), h.dtype).at[dest].set(h[tok])
    blk_expert = jnp.minimum(jnp.searchsorted(pad_end, jnp.arange(n_blocks) * MOE_BLOCK, side="right"), N_EXPERTS - 1)

    def expert_block(args):
        xb, e = args
        return (jax.nn.silu(xb @ w_gate[e]) * (xb @ w_up[e])) @ w_down[e]

    y = lax.map(expert_block, (rows.reshape(n_blocks, MOE_BLOCK, d), blk_expert)).reshape(-1, d)
    contrib = y[dest] * gate.reshape(-1)[order][:, None].astype(y.dtype)
    return jnp.zeros_like(h).at[tok].add(contrib)


def setup_inputs(seed: int = 0) -> dict:
    key = jax.random.key(seed)
    ks = jax.random.split(key, 32)
    f32 = jnp.float32
    D = D_MODEL

    def nrm(k, shape, scale):
        return jax.random.normal(k, shape, f32) * scale

    dt = jnp.exp(jax.random.uniform(ks[19], (DEPTH, 2, C_HEADS), f32, math.log(1e-3), math.log(1e-1)))
    return {
        "x": nrm(ks[0], (BATCH, SEQ, D), 1.0),
        "c": nrm(ks[1], (BATCH, D), 1.0),
        "ctx": nrm(ks[2], (BATCH, CTX_LEN, D), 1.0),
        "c_ctx": nrm(ks[3], (D,), 1.0),
        "w_mod": nrm(ks[4], (DEPTH, D, N_MOD * D), 0.5 * D ** -0.5),
        "b_mod": nrm(ks[5], (DEPTH, N_MOD * D), 0.02),
        "norm1_w": 1.0 + nrm(ks[6], (DEPTH, D), 0.05),
        "norm2_w": 1.0 + nrm(ks[7], (DEPTH, D), 0.05),
        "w_in": nrm(ks[8], (DEPTH, D, IN_COLS), D ** -0.5),
        "qn_w": 1.0 + nrm(ks[9], (DEPTH, A_HEAD_DIM), 0.05),
        "kn_w": 1.0 + nrm(ks[10], (DEPTH, A_HEAD_DIM), 0.05),
        "lam_p": nrm(ks[11], (DEPTH, 4, A_HEAD_DIM), 0.1),
        "subln_w": 1.0 + nrm(ks[12], (DEPTH, A_V_DIM), 0.05),
        "conv_dw_w": nrm(ks[13], (DEPTH, CONV_K, CONV_CH), CONV_K ** -0.5),
        "conv_dw_b": nrm(ks[14], (DEPTH, CONV_CH), 0.02),
        "conv_ln_w": 1.0 + nrm(ks[15], (DEPTH, CONV_CH), 0.05),
        "conv_ln_b": nrm(ks[16], (DEPTH, CONV_CH), 0.02),
        "gdn_conv_w": nrm(ks[17], (DEPTH, SHORT_K, 3 * C_WIDTH), SHORT_K ** -0.5),
        "gdn_a_log": jnp.log(jax.random.uniform(ks[18], (DEPTH, 2, C_HEADS), f32, 1.0, 16.0)),
        "gdn_dt_bias": jnp.log(jnp.expm1(dt)),
        "gdn_norm_w": 1.0 + nrm(ks[20], (DEPTH, C_HEAD_DIM), 0.05),
        "w_branch": nrm(ks[21], (DEPTH, N_BRANCH, BRANCH_W, D), BRANCH_W ** -0.5),
        "w_out": nrm(ks[22], (DEPTH, D, D), D ** -0.5),
        "router_w": nrm(ks[23], (D, N_EXPERTS), D ** -0.5),
        "router_bias": nrm(ks[24], (N_EXPERTS,), 0.01),
        "w_gate": nrm(ks[25], (DEPTH, N_EXPERTS, D, D_EXPERT), D ** -0.5),
        "w_up": nrm(ks[26], (DEPTH, N_EXPERTS, D, D_EXPERT), D ** -0.5),
        "w_down": nrm(ks[27], (DEPTH, N_EXPERTS, D_EXPERT, D), D_EXPERT ** -0.5),
    }


def reference(x, c, ctx, c_ctx, w_mod, b_mod, norm1_w, norm2_w, w_in, qn_w, kn_w, lam_p, subln_w,
              conv_dw_w, conv_dw_b, conv_ln_w, conv_ln_b, gdn_conv_w, gdn_a_log, gdn_dt_bias, gdn_norm_w,
              w_branch, w_out, router_w, router_bias, w_gate, w_up, w_down):
    b, s, d = x.shape
    rope = axial_rope_tables(s)
    silu_c = jax.nn.silu(c)
    silu_cc = jax.nn.silu(c_ctx)
    h_ctx = ctx
    for l in range(DEPTH):
        need_ctx = l < DEPTH - 1
        lambda_init = 0.8 - 0.6 * math.exp(-0.3 * l)
        sh1, sc1, g1, sh2, sc2, g2 = [m[:, None, :] for m in jnp.split(silu_c @ w_mod[l] + b_mod[l], N_MOD, axis=-1)]
        csh1, csc1, cg1, csh2, csc2, cg2 = jnp.split(silu_cc @ w_mod[l] + b_mod[l], N_MOD, axis=-1)

        p_lat = modulate(x, norm1_w[l], sh1, sc1) @ w_in[l]
        w_in_ctx = w_in[l] if need_ctx else w_in[l][:, :CTX_COLS]
        p_ctx = modulate(h_ctx, norm1_w[l], csh1, csc1) @ w_in_ctx
        o_a, o_c, o_b = A_COLS, A_COLS + C_COLS, A_COLS + C_COLS + B_COLS
        ya_l, ya_c = diff_attn_branch(p_lat[..., :o_a], p_ctx[..., :o_a], rope, lambda_init, need_ctx,
                                      qn_w[l], kn_w[l], lam_p[l], subln_w[l])
        yc_l, yc_c = gdn_branch(p_lat[..., o_a:o_c], p_ctx[..., o_a:o_c], need_ctx,
                                gdn_conv_w[l], gdn_a_log[l], gdn_dt_bias[l], gdn_norm_w[l])
        yb_l = conformer_conv(p_lat[..., o_c:o_b], conv_dw_w[l], conv_dw_b[l], conv_ln_w[l], conv_ln_b[l])
        x = x + g1 * merge_branches((ya_l, yb_l, yc_l), p_lat[..., o_b:], w_branch[l], w_out[l])
        if need_ctx:
            yb_c = conformer_conv(p_ctx[..., o_c:o_b], conv_dw_w[l], conv_dw_b[l], conv_ln_w[l], conv_ln_b[l])
            h_ctx = h_ctx + cg1 * merge_branches((ya_c, yb_c, yc_c), p_ctx[..., o_b:], w_branch[l], w_out[l])

        h2 = modulate(x, norm2_w[l], sh2, sc2).reshape(-1, d)
        if need_ctx:
            h2c = modulate(h_ctx, norm2_w[l], csh2, csc2).reshape(-1, d)
            f = moe_ffn(jnp.concatenate([h2, h2c], axis=0), router_w, router_bias, w_gate[l], w_up[l], w_down[l])
            x = x + g2 * f[:b * s].reshape(b, s, d)
            h_ctx = h_ctx + cg2 * f[b * s:].reshape(b, -1, d)
        else:
            x = x + g2 * moe_ffn(h2, router_w, router_bias, w_gate[l], w_up[l], w_down[l]).reshape(b, s, d)
    return x
```

```python
import functools
import math

import jax
import jax.numpy as jnp
from jax import lax
from jax.experimental import pallas as pl
from jax.experimental.pallas import tpu as pltpu

F32 = jnp.float32
BF16 = jnp.bfloat16
HIGHEST = lax.Precision.HIGHEST

D_MODEL = 1024
GRID_W = 64
N_MOD = 6
EPS = 1e-6

A_HEADS = 4
A_HEAD_DIM = 64
A_V_DIM = 2 * A_HEAD_DIM
A_WIDTH = A_HEADS * A_V_DIM
ROPE_THETA = 10000.0
SUBLN_EPS = 1e-5

CONV_CH = 512
CONV_K = 31

C_HEADS = 4
C_HEAD_DIM = 128
C_WIDTH = C_HEADS * C_HEAD_DIM
SHORT_K = 5
GDN_CHUNK = 64

N_BRANCH = 3
BRANCH_W = 512

A_COLS = 3 * A_WIDTH
C_COLS = 4 * C_WIDTH + 4 * C_HEADS
B_COLS = 2 * CONV_CH
GATE_COLS = N_BRANCH * D_MODEL

N_EXPERTS = 16
N_GROUPS = 4
EXPERTS_PER_GROUP = N_EXPERTS // N_GROUPS
TOP_K = 2
D_EXPERT = 1024
N_PAIR = EXPERTS_PER_GROUP * (EXPERTS_PER_GROUP - 1) // 2
N_CLASS = N_GROUPS * N_PAIR

LANES = 128
HALO = 16
CONV_TM = 256
ROW_CHUNK = 32
VMEM_LIMIT = 52 * 1024 * 1024

P_COLS = GATE_COLS + A_COLS + 4 * C_WIDTH + B_COLS
PB_AQ, PB_AK, PB_AV = 6, 7, 8
PB_CQKV = 3
PB_CZ, PB_BA, PB_BG = 12, 13, 14
SMALL_W = LANES


def _cparams(sem):
    return pltpu.CompilerParams(dimension_semantics=sem, vmem_limit_bytes=VMEM_LIMIT)


def _silu(x):
    return x * jax.nn.sigmoid(x)


def _pick(n, prefs):
    for p in prefs:
        if n % p == 0:
            return p
    raise ValueError(f"no tile in {prefs} divides {n}")


def _mod_body(cv_ref, w_ref, b_ref, o_ref):
    s = _silu(cv_ref[...])
    o_ref[...] = jnp.dot(s, w_ref[...], preferred_element_type=F32, precision=HIGHEST) + b_ref[...]


def _modulation(cvec, w_mod, b_mod):
    depth, d, n = w_mod.shape
    r = cvec.shape[0]
    tn = _pick(n, (1536, 1024, 512, 128))
    return pl.pallas_call(
        _mod_body,
        grid=(depth, n // tn),
        in_specs=[
            pl.BlockSpec((r, d), lambda l, j: (0, 0)),
            pl.BlockSpec((None, d, tn), lambda l, j: (l, 0, j)),
            pl.BlockSpec((None, 1, tn), lambda l, j: (l, 0, j)),
        ],
        out_specs=pl.BlockSpec((None, r, tn), lambda l, j: (l, 0, j)),
        out_shape=jax.ShapeDtypeStruct((depth, r, n), F32),
        compiler_params=_cparams(("arbitrary", "arbitrary")),
        name="modulation",
    )(cvec, w_mod, b_mod.reshape(depth, 1, n))


def _rows_are_ctx(tile_idx, tm, s_len):
    rows = tile_idx * tm + lax.broadcasted_iota(jnp.int32, (tm, 1), 0)
    return rows >= s_len


def _pick_mod(mod_ref, is_ctx):
    return jnp.where(is_ctx, mod_ref[1:2, :], mod_ref[0:1, :])


def _rms_modulate(x, w, shift, scale):
    ms = jnp.mean(x * x, axis=-1, keepdims=True)
    return x * lax.rsqrt(ms + EPS) * w * (1.0 + scale) + shift


def _inproj_body(x_ref, nw_ref, sh_ref, sc_ref, w_ref, ws_ref, p_ref, small_ref, h_scr, *, tm, s_len):
    i = pl.program_id(1)
    j = pl.program_id(2)

    @pl.when(j == 0)
    def _():
        is_ctx = _rows_are_ctx(i, tm, s_len)
        h = _rms_modulate(x_ref[...], nw_ref[...], _pick_mod(sh_ref, is_ctx), _pick_mod(sc_ref, is_ctx))
        h = h.astype(BF16)
        h_scr[...] = h
        small_ref[...] = jnp.dot(h, ws_ref[...], preferred_element_type=F32)

    p_ref[...] = jnp.dot(h_scr[...], w_ref[...], preferred_element_type=F32).astype(BF16)


def _in_projection(xs, norm_w, modb, w_main, w_small, s_len):
    b, t, d = xs.shape
    n = w_main.shape[1]
    tm = _pick(t, (768, 512, 256))
    tn = _pick(n, (1536, 512))
    body = functools.partial(_inproj_body, tm=tm, s_len=s_len)
    return pl.pallas_call(
        body,
        grid=(b, t // tm, n // tn),
        in_specs=[
            pl.BlockSpec((None, tm, d), lambda bi, i, j: (bi, i, 0)),
            pl.BlockSpec((1, d), lambda bi, i, j: (0, 0)),
            pl.BlockSpec((None, 2, d), lambda bi, i, j: (bi, 0, 0)),
            pl.BlockSpec((None, 2, d), lambda bi, i, j: (bi, 0, 1)),
            pl.BlockSpec((d, tn), lambda bi, i, j: (0, j)),
            pl.BlockSpec((d, SMALL_W), lambda bi, i, j: (0, 0)),
        ],
        out_specs=[
            pl.BlockSpec((None, tm, tn), lambda bi, i, j: (bi, i, j)),
            pl.BlockSpec((None, tm, SMALL_W), lambda bi, i, j: (bi, i, 0)),
        ],
        out_shape=[
            jax.ShapeDtypeStruct((b, t, n), BF16),
            jax.ShapeDtypeStruct((b, t, SMALL_W), F32),
        ],
        scratch_shapes=[pltpu.VMEM((tm, d), BF16)],
        compiler_params=_cparams(("arbitrary", "arbitrary", "arbitrary")),
        name="in_projection",
    )(xs, norm_w.reshape(1, d), modb, modb, w_main, w_small)


def _group_mean(x2, bd):
    hi = x2.astype(BF16)
    lo = (x2 - hi.astype(F32)).astype(BF16)
    return (jnp.dot(hi, bd, preferred_element_type=F32) + jnp.dot(lo, bd, preferred_element_type=F32))


def _rope_partner(y):
    n = y.shape[-1]
    lane = lax.broadcasted_iota(jnp.int32, y.shape, 1)
    return jnp.where((lane & 16) == 0, pltpu.roll(y, n - 16, 1), pltpu.roll(y, 16, 1))


def _attnprep_body(q_ref, k_ref, rc_ref, rs_ref, qw_ref, kw_ref, bd_ref, qo_ref, ko_ref):
    bd = bd_ref[...]
    rc = rc_ref[...]
    rs = rs_ref[...]

    def prep(x_ref, w_ref, post):
        x = x_ref[...].astype(F32)
        y = x * lax.rsqrt(_group_mean(x * x, bd) + EPS) * w_ref[...]
        y = y * rc + _rope_partner(y) * rs
        return (y * post).astype(BF16)

    qo_ref[...] = prep(q_ref, qw_ref, A_HEAD_DIM ** -0.5)
    ko_ref[...] = prep(k_ref, kw_ref, 1.0)


def _attn_prep(p, rope_c, rope_s, qn_w, kn_w):
    b, t, _ = p.shape
    w = A_WIDTH
    tm = _pick(t, (768, 512, 256))
    g = jnp.arange(w) // A_HEAD_DIM
    bd = jnp.where(g[:, None] == g[None, :], 1.0 / A_HEAD_DIM, 0.0).astype(BF16)
    reps = w // A_HEAD_DIM
    return pl.pallas_call(
        _attnprep_body,
        grid=(b, t // tm),
        in_specs=[
            pl.BlockSpec((None, tm, w), lambda bi, i: (bi, i, PB_AQ)),
            pl.BlockSpec((None, tm, w), lambda bi, i: (bi, i, PB_AK)),
            pl.BlockSpec((tm, w), lambda bi, i: (i, 0)),
            pl.BlockSpec((tm, w), lambda bi, i: (i, 0)),
            pl.BlockSpec((1, w), lambda bi, i: (0, 0)),
            pl.BlockSpec((1, w), lambda bi, i: (0, 0)),
            pl.BlockSpec((w, w), lambda bi, i: (0, 0)),
        ],
        out_specs=[
            pl.BlockSpec((None, tm, w), lambda bi, i: (bi, i, 0)),
            pl.BlockSpec((None, tm, w), lambda bi, i: (bi, i, 0)),
        ],
        out_shape=[jax.ShapeDtypeStruct((b, t, w), BF16)] * 2,
        compiler_params=_cparams(("arbitrary", "arbitrary")),
        name="attn_prep",
    )(p, p, rope_c, rope_s, jnp.tile(qn_w, reps).reshape(1, w), jnp.tile(kn_w, reps).reshape(1, w), bd)


def _attn_body(lam_ref, q_ref, k_ref, v_ref, sw_ref, o_ref, m_scr, l_scr, acc_scr, *, tq, tk, kv0, nkv, lam_init):
    q = q_ref[...]
    lane = lax.broadcasted_iota(jnp.int32, q.shape, 1)
    zero = jnp.zeros_like(q)
    qm = (jnp.where(lane < A_HEAD_DIM, q, zero), jnp.where(lane >= A_HEAD_DIM, q, zero))
    m_scr[...] = jnp.full(m_scr.shape, -jnp.inf, F32)
    l_scr[...] = jnp.zeros(l_scr.shape, F32)
    acc_scr[...] = jnp.zeros(acc_scr.shape, F32)

    def step(c, carry):
        off = pl.multiple_of(kv0 + c * tk, tk)
        kc = k_ref[pl.ds(off, tk), :]
        vc = v_ref[pl.ds(off, tk), :]
        for mi in range(2):
            s = lax.dot_general(qm[mi], kc, (((1,), (1,)), ((), ())), preferred_element_type=F32)
            m_old = m_scr[mi]
            m_new = jnp.maximum(m_old, jnp.max(s, axis=-1, keepdims=True))
            p = jnp.exp(s - m_new)
            alpha = jnp.exp(m_old - m_new)
            l_scr[mi] = alpha * l_scr[mi] + jnp.sum(p, axis=-1, keepdims=True)
            acc_scr[mi] = alpha * acc_scr[mi] + jnp.dot(p.astype(BF16), vc, preferred_element_type=F32)
            m_scr[mi] = m_new
        return carry

    lax.fori_loop(0, nkv, step, 0)

    lp = lam_ref[...]
    lam = (jnp.exp(jnp.sum(lp[0:1] * lp[1:2], axis=-1, keepdims=True))
           - jnp.exp(jnp.sum(lp[2:3] * lp[3:4], axis=-1, keepdims=True)) + lam_init)
    o = acc_scr[0] / l_scr[0] - lam * (acc_scr[1] / l_scr[1])
    ms = jnp.mean(o * o, axis=-1, keepdims=True)
    o_ref[...] = (o * lax.rsqrt(ms + SUBLN_EPS) * sw_ref[...] * (1.0 - lam_init)).astype(o_ref.dtype)


def _attention_call(lam_p, q, k, v_arr, v_blk, subln_w, *, q0, nq_rows, tq, kv0, nkv_rows, tk, lam_init, name):
    b, t, _ = q.shape
    hd = A_V_DIM
    assert q0 % tq == 0 and nq_rows % tq == 0 and kv0 % tk == 0 and nkv_rows % tk == 0
    qb0 = q0 // tq
    body = functools.partial(_attn_body, tq=tq, tk=tk, kv0=kv0, nkv=nkv_rows // tk, lam_init=lam_init)
    return pl.pallas_call(
        body,
        grid=(b, A_HEADS, nq_rows // tq),
        in_specs=[
            pl.BlockSpec((4, A_HEAD_DIM), lambda bi, h, i: (0, 0)),
            pl.BlockSpec((None, tq, hd), lambda bi, h, i: (bi, qb0 + i, h)),
            pl.BlockSpec((None, t, hd), lambda bi, h, i: (bi, 0, h)),
            pl.BlockSpec((None, t, hd), lambda bi, h, i: (bi, 0, v_blk + h)),
            pl.BlockSpec((1, hd), lambda bi, h, i: (0, 0)),
        ],
        out_specs=pl.BlockSpec((None, tq, hd), lambda bi, h, i: (bi, i, h)),
        out_shape=jax.ShapeDtypeStruct((b, nq_rows, A_WIDTH), BF16),
        scratch_shapes=[pltpu.VMEM((2, tq, 1), F32), pltpu.VMEM((2, tq, 1), F32), pltpu.VMEM((2, tq, hd), F32)],
        compiler_params=_cparams(("arbitrary", "arbitrary", "arbitrary")),
        name=name,
    )(lam_p, q, k, v_arr, subln_w.reshape(1, hd))


def _diff_attention(lam_p, q, k, p, subln_w, s_len, lam_init):
    b, t, _ = q.shape
    ctx_len = t - s_len
    tq = _pick(s_len, (512, 256))
    tk = _pick(t, (768, 256))
    v_blk = PB_AV * (A_WIDTH // A_V_DIM)
    ya_lat = _attention_call(lam_p, q, k, p, v_blk, subln_w, q0=0, nq_rows=s_len, tq=tq,
                             kv0=0, nkv_rows=t, tk=tk, lam_init=lam_init, name="diff_attention_lat")
    tc = _pick(ctx_len, (256, 128))
    assert s_len % tc == 0
    ya_ctx = _attention_call(lam_p, q, k, p, v_blk, subln_w, q0=s_len, nq_rows=ctx_len, tq=tc,
                             kv0=s_len, nkv_rows=ctx_len, tk=tc, lam_init=lam_init, name="diff_attention_ctx")
    return ya_lat, ya_ctx


def _halo_specs(tm, t, width, col_blk):
    per = tm // HALO
    last = t // HALO - 1
    main = pl.BlockSpec((None, tm, width), lambda bi, i: (bi, i, col_blk))
    prev = pl.BlockSpec((None, HALO, width), lambda bi, i: (bi, jnp.maximum(i * per - 1, 0), col_blk))
    nxt = pl.BlockSpec((None, HALO, width), lambda bi, i: (bi, jnp.minimum((i + 1) * per, last), col_blk))
    return main, prev, nxt


def _halo_ok(i, s_tiles, n_tiles):
    prev_ok = jnp.logical_and(i != 0, i != s_tiles)
    next_ok = jnp.logical_and(i != s_tiles - 1, i != n_tiles - 1)
    return prev_ok, next_ok


def _conv_rows(buf, w_ref, r0, taps, c0, c1):
    base = HALO - taps // 2 + r0
    acc = jnp.zeros((ROW_CHUNK, c1 - c0), F32)
    for j in range(taps):
        acc = acc + buf[base + j:base + j + ROW_CHUNK, c0:c1] * w_ref[j:j + 1, c0:c1]
    return acc


def _conformer_body(a_ref, ap_ref, an_ref, g_ref, gp_ref, gn_ref, w_ref, b_ref, lw_ref, lb_ref, o_ref, buf,
                    *, tm, s_tiles, n_tiles):
    i = pl.program_id(1)
    prev_ok, next_ok = _halo_ok(i, s_tiles, n_tiles)

    def glu(a, g):
        return a[...].astype(F32) * jax.nn.sigmoid(g[...].astype(F32))

    buf[0:HALO, :] = jnp.where(prev_ok, glu(ap_ref, gp_ref), 0.0)
    buf[HALO:HALO + tm, :] = glu(a_ref, g_ref)
    buf[HALO + tm:2 * HALO + tm, :] = jnp.where(next_ok, glu(an_ref, gn_ref), 0.0)
    for r0 in range(0, tm, ROW_CHUNK):
        y = _conv_rows(buf, w_ref, r0, CONV_K, 0, CONV_CH) + b_ref[...]
        mu = jnp.mean(y, axis=-1, keepdims=True)
        yc = y - mu
        var = jnp.mean(yc * yc, axis=-1, keepdims=True)
        z = yc * lax.rsqrt(var + 1e-5) * lw_ref[...] + lb_ref[...]
        o_ref[r0:r0 + ROW_CHUNK, :] = _silu(z).astype(o_ref.dtype)


def _conformer(p, dw_w, dw_b, ln_w, ln_b, s_len):
    b, t, _ = p.shape
    tm = CONV_TM
    ch = CONV_CH
    a_specs = _halo_specs(tm, t, ch, PB_BA)
    g_specs = _halo_specs(tm, t, ch, PB_BG)
    vec = pl.BlockSpec((1, ch), lambda bi, i: (0, 0))
    body = functools.partial(_conformer_body, tm=tm, s_tiles=s_len // tm, n_tiles=t // tm)
    return pl.pallas_call(
        body,
        grid=(b, t // tm),
        in_specs=[*a_specs, *g_specs, pl.BlockSpec((CONV_K, ch), lambda bi, i: (0, 0)), vec, vec, vec],
        out_specs=pl.BlockSpec((None, tm, ch), lambda bi, i: (bi, i, 0)),
        out_shape=jax.ShapeDtypeStruct((b, t, ch), BF16),
        scratch_shapes=[pltpu.VMEM((tm + 2 * HALO, ch), F32)],
        compiler_params=_cparams(("arbitrary", "arbitrary")),
        name="conformer_conv",
    )(p, p, p, p, p, p, dw_w, dw_b.reshape(1, ch), ln_w.reshape(1, ch), ln_b.reshape(1, ch))


def _softplus(x):
    return jnp.maximum(x, 0.0) + jnp.log(1.0 + jnp.exp(-jnp.abs(x)))


def _gdnprep_body(x_ref, xp_ref, xn_ref, small_ref, w_ref, alog_ref, dtb_ref, q_ref, k_ref, v_ref, gb_ref, buf,
                  *, tm, s_tiles, n_tiles):
    i = pl.program_id(1)
    prev_ok, next_ok = _halo_ok(i, s_tiles, n_tiles)
    buf[0:HALO, :] = jnp.where(prev_ok, xp_ref[...].astype(F32), 0.0)
    buf[HALO:HALO + tm, :] = x_ref[...].astype(F32)
    buf[HALO + tm:2 * HALO + tm, :] = jnp.where(next_ok, xn_ref[...].astype(F32), 0.0)
    outs = (q_ref, k_ref, v_ref)
    for r0 in range(0, tm, ROW_CHUNK):
        for part in range(3):
            y = _silu(_conv_rows(buf, w_ref, r0, SHORT_K, part * C_WIDTH, (part + 1) * C_WIDTH))
            if part < 2:
                post = C_HEAD_DIM ** -0.5 if part == 0 else 1.0
                for h in range(C_HEADS):
                    yh = y[:, h * C_HEAD_DIM:(h + 1) * C_HEAD_DIM]
                    n = yh * lax.rsqrt(jnp.sum(yh * yh, axis=-1, keepdims=True) + 1e-6)
                    outs[part][r0:r0 + ROW_CHUNK, h * C_HEAD_DIM:(h + 1) * C_HEAD_DIM] = n * post
            else:
                outs[part][r0:r0 + ROW_CHUNK, :] = y
    sm = small_ref[...]
    lane = lax.broadcasted_iota(jnp.int32, sm.shape, 1)
    gdec = -jnp.exp(alog_ref[...]) * _softplus(sm + dtb_ref[...])
    gb_ref[...] = jnp.where(lane < 2 * C_HEADS, jax.nn.sigmoid(sm), gdec)


def _gdn_prep(p, small, conv_w, a_log, dt_bias, s_len):
    b, t, _ = p.shape
    tm = CONV_TM
    w3 = 3 * C_WIDTH
    x_specs = _halo_specs(tm, t, w3, PB_CQKV)
    pad = SMALL_W - 4 * C_HEADS
    alog = jnp.concatenate([jnp.zeros((2 * C_HEADS,), F32), a_log.reshape(-1), jnp.zeros((pad,), F32)]).reshape(1, SMALL_W)
    dtb = jnp.concatenate([jnp.zeros((2 * C_HEADS,), F32), dt_bias.reshape(-1), jnp.zeros((pad,), F32)]).reshape(1, SMALL_W)
    vec = pl.BlockSpec((1, SMALL_W), lambda bi, i: (0, 0))
    out_blk = pl.BlockSpec((None, tm, C_WIDTH), lambda bi, i: (bi, i, 0))
    body = functools.partial(_gdnprep_body, tm=tm, s_tiles=s_len // tm, n_tiles=t // tm)
    return pl.pallas_call(
        body,
        grid=(b, t // tm),
        in_specs=[*x_specs, pl.BlockSpec((None, tm, SMALL_W), lambda bi, i: (bi, i, 0)),
                  pl.BlockSpec((SHORT_K, w3), lambda bi, i: (0, 0)), vec, vec],
        out_specs=[out_blk, out_blk, out_blk, pl.BlockSpec((None, tm, SMALL_W), lambda bi, i: (bi, i, 0))],
        out_shape=[jax.ShapeDtypeStruct((b, t, C_WIDTH), F32)] * 3 + [jax.ShapeDtypeStruct((b, t, SMALL_W), F32)],
        scratch_shapes=[pltpu.VMEM((tm + 2 * HALO, w3), F32)],
        compiler_params=_cparams(("arbitrary", "arbitrary")),
        name="gdn_prep",
    )(p, p, p, small, conv_w, alog, dtb)


def _mm(a, b):
    return jnp.dot(a.astype(BF16), b.astype(BF16), preferred_element_type=F32)


def _mm_nt(a, b):
    return lax.dot_general(a.astype(BF16), b.astype(BF16), (((1,), (1,)), ((), ())), preferred_element_type=F32)


def _mm_tn(a, b):
    return lax.dot_general(a.astype(BF16), b.astype(BF16), (((0,), (0,)), ((), ())), preferred_element_type=F32)


def _gdn_chain(q, k, v, gcol, grow, bcol, state, rev):
    c = q.shape[0]
    ii = lax.broadcasted_iota(jnp.int32, (c, c), 0)
    jj = lax.broadcasted_iota(jnp.int32, (c, c), 1)
    incl = (ii <= jj) if rev else (ii >= jj)
    strict = (ii < jj) if rev else (ii > jj)
    eye = (ii == jj).astype(F32)
    decay = jnp.where(incl, jnp.exp(jnp.where(incl, gcol - grow, 0.0)), 0.0)
    kb = k * bcol
    low = jnp.where(strict, _mm_nt(kb, k) * decay, 0.0)
    pw = -low
    t_inv = eye + pw
    for _ in range(int(math.log2(c)) - 1):
        pw = _mm(pw, pw)
        t_inv = t_inv + _mm(t_inv, pw)
    eg = jnp.exp(gcol)
    u = _mm(t_inv, v * bcol)
    w = _mm(t_inv, kb * eg)
    g_last = gcol[0:1, :] if rev else gcol[c - 1:c, :]
    k_dec = k * jnp.exp(g_last - gcol)
    q_dec = q * eg
    intra = jnp.where(incl, _mm_nt(q, k) * decay, 0.0)
    v_new = u - _mm(w, state)
    o = _mm(q_dec, state) + _mm(intra, v_new)
    new_state = state * jnp.exp(g_last) + _mm_tn(k_dec, v_new)
    return o, new_state


def _gdn_body(qf_ref, kf_ref, vf_ref, gf_ref, qb_ref, kb_ref, vb_ref, gbb_ref, of_ref, ob_ref, st_scr):
    @pl.when(pl.program_id(1) == 0)
    def _():
        st_scr[...] = jnp.zeros(st_scr.shape, F32)

    c = GDN_CHUNK
    ii = lax.broadcasted_iota(jnp.int32, (c, c), 0)
    jj = lax.broadcasted_iota(jnp.int32, (c, c), 1)
    dirs = ((qf_ref, kf_ref, vf_ref, gf_ref, of_ref, False), (qb_ref, kb_ref, vb_ref, gbb_ref, ob_ref, True))
    for d, (q_ref, k_ref, v_ref, g_ref, o_ref, rev) in enumerate(dirs):
        gb = g_ref[...]
        tri = ((ii <= jj) if rev else (ii >= jj)).astype(F32)
        gc = jnp.dot(tri, gb, preferred_element_type=F32, precision=HIGHEST)
        gct = gc.T
        for h in range(C_HEADS):
            sl = slice(h * C_HEAD_DIM, (h + 1) * C_HEAD_DIM)
            bc = d * C_HEADS + h
            gcn = 2 * C_HEADS + bc
            o, st = _gdn_chain(q_ref[:, sl], k_ref[:, sl], v_ref[:, sl], gc[:, gcn:gcn + 1], gct[gcn:gcn + 1, :],
                               gb[:, bc:bc + 1], st_scr[bc], rev)
            o_ref[:, sl] = o
            st_scr[bc] = st


def _gdn_scan(q, k, v, gb, s_len):
    b, t, w = q.shape
    c = GDN_CHUNK
    n_lat = s_len // c
    n_all = t // c
    n_ctx = n_all - n_lat

    def fwd(bi, s):
        return (bi, jnp.where(s < n_ctx, n_lat + s, s - n_ctx), 0)

    def bwd(bi, s):
        return (bi, jnp.where(s < n_ctx, n_all - 1 - s, n_all - 1 - s), 0)

    wide_f = pl.BlockSpec((None, c, w), fwd)
    wide_b = pl.BlockSpec((None, c, w), bwd)
    nar_f = pl.BlockSpec((None, c, SMALL_W), fwd)
    nar_b = pl.BlockSpec((None, c, SMALL_W), bwd)
    return pl.pallas_call(
        _gdn_body,
        grid=(b, n_all),
        in_specs=[wide_f, wide_f, wide_f, nar_f, wide_b, wide_b, wide_b, nar_b],
        out_specs=[wide_f, wide_b],
        out_shape=[jax.ShapeDtypeStruct((b, t, w), F32)] * 2,
        scratch_shapes=[pltpu.VMEM((2 * C_HEADS, C_HEAD_DIM, C_HEAD_DIM), F32)],
        compiler_params=_cparams(("arbitrary", "arbitrary")),
        name="gdn_scan",
    )(q, k, v, gb, q, k, v, gb)


def _merge_body(x_ref, yal_ref, yac_ref, yb_ref, of_ref, ob_ref, z_ref, g0_ref, g1_ref, g2_ref, gnw_ref, wb_ref,
                wo_ref, gate_ref, n2_ref, sh_ref, sc_ref, rw_ref, xo_ref, h2_ref, sco_ref, *, tm, s_len):
    i = pl.program_id(1)
    is_ctx = _rows_are_ctx(i, tm, s_len)
    ya = jnp.where(i * tm >= s_len, yac_ref[...], yal_ref[...])
    o = of_ref[...] + ob_ref[...]
    z = z_ref[...].astype(F32)
    parts = []
    for h in range(C_HEADS):
        sl = slice(h * C_HEAD_DIM, (h + 1) * C_HEAD_DIM)
        oh = o[:, sl]
        ms = jnp.mean(oh * oh, axis=-1, keepdims=True)
        parts.append((oh * lax.rsqrt(ms + EPS) * gnw_ref[...] * _silu(z[:, sl])).astype(BF16))
    yc = jnp.concatenate(parts, axis=-1)
    m = jnp.zeros((tm, D_MODEL), F32)
    for bi, (y, g_ref) in enumerate(((ya, g0_ref), (yb_ref[...], g1_ref), (yc, g2_ref))):
        m = m + jax.nn.sigmoid(g_ref[...].astype(F32)) * jnp.dot(y, wb_ref[bi], preferred_element_type=F32)
    upd = jnp.dot(m.astype(BF16), wo_ref[...], preferred_element_type=F32)
    x = x_ref[...] + _pick_mod(gate_ref, is_ctx) * upd
    xo_ref[...] = x
    h2 = _rms_modulate(x, n2_ref[...], _pick_mod(sh_ref, is_ctx), _pick_mod(sc_ref, is_ctx))
    h2_ref[...] = h2
    sco_ref[...] = jax.nn.sigmoid(jnp.dot(h2, rw_ref[...], preferred_element_type=F32, precision=HIGHEST))


def _merge(xs, ya_lat, ya_ctx, yb, o_f, o_b, p, gdn_norm_w, w_branch, w_out, modb, norm2_w, router_w, s_len):
    b, t, d = xs.shape
    tm = CONV_TM
    bw = BRANCH_W
    e_pad = LANES
    s_tiles = s_len // tm
    n_tiles = t // tm
    lat_spec = pl.BlockSpec((None, tm, bw), lambda bi, i: (bi, jnp.minimum(i, s_tiles - 1), 0))
    ctx_spec = pl.BlockSpec((None, tm, bw), lambda bi, i: (bi, jnp.maximum(i - s_tiles, 0), 0))
    rw = jnp.zeros((d, e_pad), F32).at[:, :N_EXPERTS].set(router_w)
    tok = lambda width, blk: pl.BlockSpec((None, tm, width), lambda bi, i: (bi, i, blk))
    modspec = lambda k: pl.BlockSpec((None, 2, d), lambda bi, i: (bi, 0, k))
    full = lambda shape: pl.BlockSpec(shape, lambda bi, i: (0,) * len(shape))
    body = functools.partial(_merge_body, tm=tm, s_len=s_len)
    return pl.pallas_call(
        body,
        grid=(b, t // tm),
        in_specs=[
            tok(d, 0), lat_spec, ctx_spec, tok(bw, 0), tok(bw, 0), tok(bw, 0), tok(bw, PB_CZ),
            tok(d, 0), tok(d, 1), tok(d, 2),
            full((1, C_HEAD_DIM)), full((N_BRANCH, bw, d)), full((d, d)),
            modspec(2), full((1, d)), modspec(3), modspec(4), full((d, e_pad)),
        ],
        out_specs=[tok(d, 0), tok(d, 0), tok(e_pad, 0)],
        out_shape=[jax.ShapeDtypeStruct((b, t, d), F32), jax.ShapeDtypeStruct((b, t, d), F32),
                   jax.ShapeDtypeStruct((b, t, e_pad), F32)],
        compiler_params=_cparams(("arbitrary", "arbitrary")),
        name="merge_norm_router",
    )(xs, ya_lat, ya_ctx, yb, o_f, o_b, p, p, p, p, gdn_norm_w.reshape(1, C_HEAD_DIM), w_branch, w_out,
      modb, norm2_w.reshape(1, d), modb, modb, rw)


MOE_ROWS = 256


def _routing_tables(scores, router_bias):
    n = scores.shape[0]
    rb = MOE_ROWS
    grp = (scores + router_bias.astype(F32)).reshape(n, N_GROUPS, EXPERTS_PER_GROUP)
    best = jnp.argmax(lax.top_k(grp, TOP_K)[0].sum(-1), axis=-1)
    in_grp = jnp.take_along_axis(grp, best[:, None, None], axis=1)[:, 0]
    _, local = lax.top_k(in_grp, TOP_K)
    expert = best[:, None] * EXPERTS_PER_GROUP + local
    w = jnp.take_along_axis(scores, expert, axis=-1)
    gate = w / jnp.sum(w, axis=-1, keepdims=True)
    lo = jnp.min(local, axis=-1)
    hi = jnp.max(local, axis=-1)
    first_is_lo = local[:, 0] < local[:, 1]
    g_lo = jnp.where(first_is_lo, gate[:, 0], gate[:, 1])
    g_hi = jnp.where(first_is_lo, gate[:, 1], gate[:, 0])
    pair = lo * (2 * EXPERTS_PER_GROUP - lo - 1) // 2 + (hi - lo - 1)
    cls = (best * N_PAIR + pair).astype(jnp.int32)

    order = jnp.argsort(cls)
    sorted_cls = cls[order]
    counts = jnp.zeros((N_CLASS,), jnp.int32).at[cls].add(1)
    padded = (counts + rb - 1) // rb * rb
    pad_end = jnp.cumsum(padded)
    seg_start = jnp.cumsum(counts) - counts
    dest = (pad_end - padded)[sorted_cls] + jnp.arange(n, dtype=jnp.int32) - seg_start[sorted_cls]
    n_blocks = n // rb + N_CLASS
    n_rows = n_blocks * rb
    row_tok = jnp.zeros((n_rows,), jnp.int32).at[dest].set(order.astype(jnp.int32))
    tok_row = jnp.zeros((n,), jnp.int32).at[order].set(dest.astype(jnp.int32))
    row_gate = jnp.zeros((n_rows, LANES), F32).at[dest, 0].set(g_lo[order]).at[dest, 1].set(g_hi[order])
    n_used = (pad_end[-1] // rb).astype(jnp.int32)
    blk = jnp.minimum(jnp.arange(n_blocks, dtype=jnp.int32), n_used - 1)
    blk_cls = jnp.minimum(jnp.searchsorted(pad_end, blk * rb, side="right"), N_CLASS - 1).astype(jnp.int32)
    pl_lo, pl_hi = [], []
    for a in range(EXPERTS_PER_GROUP):
        for c in range(a + 1, EXPERTS_PER_GROUP):
            pl_lo.append(a)
            pl_hi.append(c)
    pl_lo = jnp.asarray(pl_lo, jnp.int32)
    pl_hi = jnp.asarray(pl_hi, jnp.int32)
    blk_grp = blk_cls // N_PAIR
    blk_pair = blk_cls % N_PAIR
    e_lo = blk_grp * EXPERTS_PER_GROUP + pl_lo[blk_pair]
    e_hi = blk_grp * EXPERTS_PER_GROUP + pl_hi[blk_pair]
    return e_lo, e_hi, n_used.reshape(1), row_tok, tok_row, row_gate


def _gather_rows(idx_ref, base, n_rows, src_hbm, dst, sem):
    def issue(r, carry):
        pltpu.make_async_copy(src_hbm.at[pl.ds(idx_ref[base + r], 1), :], dst.at[pl.ds(r, 1), :], sem).start()
        return carry

    lax.fori_loop(0, n_rows, issue, 0)
    pltpu.make_async_copy(src_hbm.at[pl.ds(0, n_rows), :], dst, sem).wait()


def _moe_body(elo_ref, ehi_ref, nused_ref, rowtok_ref, h2_hbm, gate_ref, wg_lo, wu_lo, wd_lo, wg_hi, wu_hi, wd_hi,
              y_ref, xbuf, sem, *, rb):
    i = pl.program_id(0)

    @pl.when(i < nused_ref[0])
    def _():
        _gather_rows(rowtok_ref, i * rb, rb, h2_hbm, xbuf, sem)
        x = xbuf[...].astype(BF16)
        gate = gate_ref[...]

        def ffn(wg, wu, wd):
            hg = jnp.dot(x, wg[...], preferred_element_type=F32)
            hu = jnp.dot(x, wu[...], preferred_element_type=F32)
            return jnp.dot((_silu(hg) * hu).astype(BF16), wd[...], preferred_element_type=F32)

        y_ref[...] = gate[:, 0:1] * ffn(wg_lo, wu_lo, wd_lo) + gate[:, 1:2] * ffn(wg_hi, wu_hi, wd_hi)

    @pl.when(i >= nused_ref[0])
    def _():
        y_ref[...] = jnp.zeros(y_ref.shape, F32)


def _moe_ffn(h2_flat, tables, w_gate, w_up, w_down):
    e_lo, e_hi, n_used, row_tok, _, row_gate = tables
    n, d = h2_flat.shape
    rb = MOE_ROWS
    n_blocks = e_lo.shape[0]
    de = w_gate.shape[-1]
    w_in_lo = pl.BlockSpec((None, d, de), lambda i, lo, hi, nu, rt: (lo[i], 0, 0))
    w_in_hi = pl.BlockSpec((None, d, de), lambda i, lo, hi, nu, rt: (hi[i], 0, 0))
    w_dn_lo = pl.BlockSpec((None, de, d), lambda i, lo, hi, nu, rt: (lo[i], 0, 0))
    w_dn_hi = pl.BlockSpec((None, de, d), lambda i, lo, hi, nu, rt: (hi[i], 0, 0))
    grid_spec = pltpu.PrefetchScalarGridSpec(
        num_scalar_prefetch=4,
        grid=(n_blocks,),
        in_specs=[
            pl.BlockSpec(memory_space=pl.ANY),
            pl.BlockSpec((rb, LANES), lambda i, lo, hi, nu, rt: (i, 0)),
            w_in_lo, w_in_lo, w_dn_lo, w_in_hi, w_in_hi, w_dn_hi,
        ],
        out_specs=pl.BlockSpec((rb, d), lambda i, lo, hi, nu, rt: (i, 0)),
        scratch_shapes=[pltpu.VMEM((rb, d), F32), pltpu.SemaphoreType.DMA(())],
    )
    return pl.pallas_call(
        functools.partial(_moe_body, rb=rb),
        grid_spec=grid_spec,
        out_shape=jax.ShapeDtypeStruct((n_blocks * rb, d), F32),
        compiler_params=_cparams(("arbitrary",)),
        name="moe_expert_ffn",
    )(e_lo, e_hi, n_used, row_tok, h2_flat, row_gate, w_gate, w_up, w_down, w_gate, w_up, w_down)


def _combine_body(tokrow_ref, x_ref, gate_ref, y_hbm, o_ref, ybuf, sem, *, tm, s_len, tiles_per_batch):
    bi = pl.program_id(0)
    i = pl.program_id(1)
    _gather_rows(tokrow_ref, (bi * tiles_per_batch + i) * tm, tm, y_hbm, ybuf, sem)
    is_ctx = _rows_are_ctx(i, tm, s_len)
    o_ref[...] = x_ref[...] + _pick_mod(gate_ref, is_ctx) * ybuf[...]


def _combine(xs, y_sorted, tok_row, modb, s_len):
    b, t, d = xs.shape
    tm = _pick(t, (256,))
    grid_spec = pltpu.PrefetchScalarGridSpec(
        num_scalar_prefetch=1,
        grid=(b, t // tm),
        in_specs=[
            pl.BlockSpec((None, tm, d), lambda bi, i, tr: (bi, i, 0)),
            pl.BlockSpec((None, 2, d), lambda bi, i, tr: (bi, 0, 5)),
            pl.BlockSpec(memory_space=pl.ANY),
        ],
        out_specs=pl.BlockSpec((None, tm, d), lambda bi, i, tr: (bi, i, 0)),
        scratch_shapes=[pltpu.VMEM((tm, d), F32), pltpu.SemaphoreType.DMA(())],
    )
    body = functools.partial(_combine_body, tm=tm, s_len=s_len, tiles_per_batch=t // tm)
    return pl.pallas_call(
        body,
        grid_spec=grid_spec,
        out_shape=jax.ShapeDtypeStruct((b, t, d), F32),
        compiler_params=_cparams(("arbitrary", "arbitrary")),
        name="moe_combine_residual",
    )(tok_row, xs, modb, y_sorted)


def _rope_tables(s_len, ctx_len):
    t = jnp.arange(s_len, dtype=jnp.int32)
    r = (t // GRID_W).astype(F32)
    col = (t % GRID_W).astype(F32)
    axis_dim = A_HEAD_DIM // 2
    inv_freq = ROPE_THETA ** (-jnp.arange(0, axis_dim, 2, dtype=F32) / axis_dim)
    ang_r = r[:, None] * inv_freq
    ang_c = col[:, None] * inv_freq
    cos = jnp.concatenate([jnp.cos(ang_r), jnp.cos(ang_r), jnp.cos(ang_c), jnp.cos(ang_c)], axis=-1)
    sin = jnp.concatenate([-jnp.sin(ang_r), jnp.sin(ang_r), -jnp.sin(ang_c), jnp.sin(ang_c)], axis=-1)
    reps = A_WIDTH // A_HEAD_DIM
    cos = jnp.concatenate([jnp.tile(cos, (1, reps)), jnp.ones((ctx_len, A_WIDTH), F32)], axis=0)
    sin = jnp.concatenate([jnp.tile(sin, (1, reps)), jnp.zeros((ctx_len, A_WIDTH), F32)], axis=0)
    return cos, sin


def _split_w_in(w_in_l):
    oa, oc, ob = A_COLS, A_COLS + C_COLS, A_COLS + C_COLS + B_COLS
    a = w_in_l[:, :oa]
    c_main = w_in_l[:, oa:oa + 4 * C_WIDTH]
    c_small = w_in_l[:, oa + 4 * C_WIDTH:oc]
    bb = w_in_l[:, oc:ob]
    gates = w_in_l[:, ob:]
    main = jnp.concatenate([gates, a, c_main, bb], axis=1).astype(BF16)
    small = jnp.zeros((w_in_l.shape[0], SMALL_W), F32).at[:, :4 * C_HEADS].set(c_small).astype(BF16)
    return main, small


def kernel(x, c, ctx, c_ctx, w_mod, b_mod, norm1_w, norm2_w, w_in, qn_w, kn_w, lam_p, subln_w, conv_dw_w, conv_dw_b,
           conv_ln_w, conv_ln_b, gdn_conv_w, gdn_a_log, gdn_dt_bias, gdn_norm_w, w_branch, w_out, router_w,
           router_bias, w_gate, w_up, w_down):
    b, s_len, d = x.shape
    ctx_len = ctx.shape[1]
    depth = w_mod.shape[0]
    t = s_len + ctx_len
    assert s_len % CONV_TM == 0 and ctx_len % CONV_TM == 0 and s_len % GRID_W == 0

    xs = jnp.concatenate([x, ctx], axis=1)
    rows = -(-(b + 1) // 8) * 8
    cvec = jnp.zeros((rows, d), F32).at[:b].set(c).at[b].set(c_ctx)
    mod = _modulation(cvec, w_mod, b_mod)
    rope_c, rope_s = _rope_tables(s_len, ctx_len)

    for l in range(depth):
        lam_init = 0.8 - 0.6 * math.exp(-0.3 * l)
        modb = jnp.stack([mod[l, :b], jnp.broadcast_to(mod[l, b], (b, N_MOD * d))], axis=1)
        w_main, w_small = _split_w_in(w_in[l])

        p, small = _in_projection(xs, norm1_w[l], modb, w_main, w_small, s_len)
        q, k = _attn_prep(p, rope_c, rope_s, qn_w[l], kn_w[l])
        ya_lat, ya_ctx = _diff_attention(lam_p[l], q, k, p, subln_w[l], s_len, lam_init)
        yb = _conformer(p, conv_dw_w[l], conv_dw_b[l], conv_ln_w[l], conv_ln_b[l], s_len)
        gq, gk, gv, gb = _gdn_prep(p, small, gdn_conv_w[l], gdn_a_log[l], gdn_dt_bias[l], s_len)
        o_f, o_b = _gdn_scan(gq, gk, gv, gb, s_len)
        xs, h2, scores = _merge(xs, ya_lat, ya_ctx, yb, o_f, o_b, p, gdn_norm_w[l], w_branch[l].astype(BF16),
                                w_out[l].astype(BF16), modb, norm2_w[l], router_w, s_len)

        tables = _routing_tables(scores.reshape(b * t, -1)[:, :N_EXPERTS], router_bias)
        y_sorted = _moe_ffn(h2.reshape(b * t, d), tables, w_gate[l].astype(BF16), w_up[l].astype(BF16),
                            w_down[l].astype(BF16))
        xs = _combine(xs, y_sorted, tables[4], modb, s_len)
    return xs[:, :s_len]
```

```python
import functools
import math

import jax
import jax.numpy as jnp
from jax import lax
from jax.experimental import pallas as pl
from jax.experimental.pallas import tpu as pltpu

F32 = jnp.float32
BF16 = jnp.bfloat16
HIGHEST = lax.Precision.HIGHEST

D_MODEL = 1024
GRID_W = 64
N_MOD = 6
EPS = 1e-6

A_HEADS = 4
A_HEAD_DIM = 64
A_V_DIM = 2 * A_HEAD_DIM
A_WIDTH = A_HEADS * A_V_DIM
ROPE_THETA = 10000.0
SUBLN_EPS = 1e-5

CONV_CH = 512
CONV_K = 31

C_HEADS = 4
C_HEAD_DIM = 128
C_WIDTH = C_HEADS * C_HEAD_DIM
SHORT_K = 5
GDN_CHUNK = 64

N_BRANCH = 3
BRANCH_W = 512

A_COLS = 3 * A_WIDTH
C_COLS = 4 * C_WIDTH + 4 * C_HEADS
B_COLS = 2 * CONV_CH
GATE_COLS = N_BRANCH * D_MODEL

N_EXPERTS = 16
N_GROUPS = 4
EXPERTS_PER_GROUP = N_EXPERTS // N_GROUPS
TOP_K = 2
D_EXPERT = 1024
N_PAIR = EXPERTS_PER_GROUP * (EXPERTS_PER_GROUP - 1) // 2
N_CLASS = N_GROUPS * N_PAIR

LANES = 128
HALO = 16
CONV_TM = 256
ROW_CHUNK = 32
VMEM_LIMIT = 52 * 1024 * 1024

P_COLS = GATE_COLS + A_COLS + 4 * C_WIDTH + B_COLS
PB_AQ, PB_AK, PB_AV = 6, 7, 8
PB_CQKV = 3
PB_CZ, PB_BA, PB_BG = 12, 13, 14
SMALL_W = LANES


def _cparams(sem):
    return pltpu.CompilerParams(dimension_semantics=sem, vmem_limit_bytes=VMEM_LIMIT)


def _silu(x):
    return x * jax.nn.sigmoid(x)


def _pick(n, prefs):
    for p in prefs:
        if n % p == 0:
            return p
    raise ValueError(f"no tile in {prefs} divides {n}")


def _mod_body(cv_ref, w_ref, b_ref, o_ref):
    s = _silu(cv_ref[...])
    o_ref[...] = jnp.dot(s, w_ref[...], preferred_element_type=F32, precision=HIGHEST) + b_ref[...]


def _modulation(cvec, w_mod, b_mod):
    depth, d, n = w_mod.shape
    r = cvec.shape[0]
    tn = _pick(n, (1536, 1024, 512, 128))
    return pl.pallas_call(
        _mod_body,
        grid=(depth, n // tn),
        in_specs=[
            pl.BlockSpec((r, d), lambda l, j: (0, 0)),
            pl.BlockSpec((None, d, tn), lambda l, j: (l, 0, j)),
            pl.BlockSpec((None, 1, tn), lambda l, j: (l, 0, j)),
        ],
        out_specs=pl.BlockSpec((None, r, tn), lambda l, j: (l, 0, j)),
        out_shape=jax.ShapeDtypeStruct((depth, r, n), F32),
        compiler_params=_cparams(("arbitrary", "arbitrary")),
        name="modulation",
    )(cvec, w_mod, b_mod.reshape(depth, 1, n))


def _rows_are_ctx(tile_idx, tm, s_len):
    rows = tile_idx * tm + lax.broadcasted_iota(jnp.int32, (tm, 1), 0)
    return rows >= s_len


def _pick_mod(mod_ref, is_ctx):
    return jnp.where(is_ctx, mod_ref[1:2, :], mod_ref[0:1, :])


def _rms_modulate(x, w, shift, scale):
    ms = jnp.mean(x * x, axis=-1, keepdims=True)
    return x * lax.rsqrt(ms + EPS) * w * (1.0 + scale) + shift


def _inproj_body(x_ref, nw_ref, sh_ref, sc_ref, w_ref, ws_ref, p_ref, small_ref, h_scr, *, tm, s_len):
    i = pl.program_id(1)
    j = pl.program_id(2)

    @pl.when(j == 0)
    def _():
        is_ctx = _rows_are_ctx(i, tm, s_len)
        h = _rms_modulate(x_ref[...], nw_ref[...], _pick_mod(sh_ref, is_ctx), _pick_mod(sc_ref, is_ctx))
        h = h.astype(BF16)
        h_scr[...] = h
        small_ref[...] = jnp.dot(h, ws_ref[...], preferred_element_type=F32)

    p_ref[...] = jnp.dot(h_scr[...], w_ref[...], preferred_element_type=F32).astype(BF16)


def _in_projection(xs, norm_w, modb, w_main, w_small, s_len):
    b, t, d = xs.shape
    n = w_main.shape[1]
    tm = _pick(t, (768, 512, 256))
    tn = _pick(n, (1536, 512))
    body = functools.partial(_inproj_body, tm=tm, s_len=s_len)
    return pl.pallas_call(
        body,
        grid=(b, t // tm, n // tn),
        in_specs=[
            pl.BlockSpec((None, tm, d), lambda bi, i, j: (bi, i, 0)),
            pl.BlockSpec((1, d), lambda bi, i, j: (0, 0)),
            pl.BlockSpec((None, 2, d), lambda bi, i, j: (bi, 0, 0)),
            pl.BlockSpec((None, 2, d), lambda bi, i, j: (bi, 0, 1)),
            pl.BlockSpec((d, tn), lambda bi, i, j: (0, j)),
            pl.BlockSpec((d, SMALL_W), lambda bi, i, j: (0, 0)),
        ],
        out_specs=[
            pl.BlockSpec((None, tm, tn), lambda bi, i, j: (bi, i, j)),
            pl.BlockSpec((None, tm, SMALL_W), lambda bi, i, j: (bi, i, 0)),
        ],
        out_shape=[
            jax.ShapeDtypeStruct((b, t, n), BF16),
            jax.ShapeDtypeStruct((b, t, SMALL_W), F32),
        ],
        scratch_shapes=[pltpu.VMEM((tm, d), BF16)],
        compiler_params=_cparams(("arbitrary", "arbitrary", "arbitrary")),
        name="in_projection",
    )(xs, norm_w.reshape(1, d), modb, modb, w_main, w_small)


def _group_mean(x2, bd):
    hi = x2.astype(BF16)
    lo = (x2 - hi.astype(F32)).astype(BF16)
    return (jnp.dot(hi, bd, preferred_element_type=F32) + jnp.dot(lo, bd, preferred_element_type=F32))


def _rope_partner(y):
    n = y.shape[-1]
    lane = lax.broadcasted_iota(jnp.int32, y.shape, 1)
    return jnp.where((lane & 16) == 0, pltpu.roll(y, n - 16, 1), pltpu.roll(y, 16, 1))


def _attnprep_body(q_ref, k_ref, v_ref, rc_ref, rs_ref, qw_ref, kw_ref, bd_ref, qt_ref, ko_ref, vt_ref):
    bd = bd_ref[...]
    rc = rc_ref[...]
    rs = rs_ref[...]

    def prep(x_ref, w_ref, post):
        x = x_ref[...].astype(F32)
        y = x * lax.rsqrt(_group_mean(x * x, bd) + EPS) * w_ref[...]
        y = y * rc + _rope_partner(y) * rs
        return y * post

    qt_ref[...] = prep(q_ref, qw_ref, A_HEAD_DIM ** -0.5 * math.log2(math.e)).T.astype(BF16)
    ko_ref[...] = prep(k_ref, kw_ref, 1.0).astype(BF16)
    vt_ref[...] = v_ref[...].astype(F32).T.astype(BF16)


def _attn_prep(p, rope_c, rope_s, qn_w, kn_w):
    b, t, _ = p.shape
    w = A_WIDTH
    tm = _pick(t, (768, 512, 256))
    g = jnp.arange(w) // A_HEAD_DIM
    bd = jnp.where(g[:, None] == g[None, :], 1.0 / A_HEAD_DIM, 0.0).astype(BF16)
    reps = w // A_HEAD_DIM
    tok = lambda blk: pl.BlockSpec((None, tm, w), lambda bi, i: (bi, i, blk))
    tr = pl.BlockSpec((None, w, tm), lambda bi, i: (bi, 0, i))
    return pl.pallas_call(
        _attnprep_body,
        grid=(b, t // tm),
        in_specs=[
            tok(PB_AQ), tok(PB_AK), tok(PB_AV),
            pl.BlockSpec((tm, w), lambda bi, i: (i, 0)),
            pl.BlockSpec((tm, w), lambda bi, i: (i, 0)),
            pl.BlockSpec((1, w), lambda bi, i: (0, 0)),
            pl.BlockSpec((1, w), lambda bi, i: (0, 0)),
            pl.BlockSpec((w, w), lambda bi, i: (0, 0)),
        ],
        out_specs=[tr, tok(0), tr],
        out_shape=[jax.ShapeDtypeStruct((b, w, t), BF16), jax.ShapeDtypeStruct((b, t, w), BF16),
                   jax.ShapeDtypeStruct((b, w, t), BF16)],
        compiler_params=_cparams(("arbitrary", "arbitrary")),
        name="attn_prep",
    )(p, p, p, rope_c, rope_s, jnp.tile(qn_w, reps).reshape(1, w), jnp.tile(kn_w, reps).reshape(1, w), bd)


def _attn_body(lam_ref, qt_ref, k_ref, vt_ref, sw_ref, o_ref, m_scr, l_scr, acc_scr, sa_scr, sb_scr,
               *, tq, tk, kv0, nkv, lam_init):
    qt = qt_ref[...]
    row = lax.broadcasted_iota(jnp.int32, qt.shape, 0)
    zero = jnp.zeros_like(qt)
    qm = (jnp.where(row < A_HEAD_DIM, qt, zero), jnp.where(row >= A_HEAD_DIM, qt, zero))
    m_scr[...] = jnp.full(m_scr.shape, -jnp.inf, F32)
    l_scr[...] = jnp.zeros(l_scr.shape, F32)
    acc_scr[...] = jnp.zeros(acc_scr.shape, F32)

    def scores(c, dst):
        off = pl.multiple_of(kv0 + c * tk, tk)
        kc = k_ref[pl.ds(off, tk), :]
        for mi in range(2):
            dst[mi] = jnp.dot(kc, qm[mi], preferred_element_type=F32)

    def accumulate(c, src):
        off = pl.multiple_of(kv0 + c * tk, tk)
        vtc = vt_ref[:, pl.ds(off, tk)]
        s = [src[mi] for mi in range(2)]
        m_old = [m_scr[mi] for mi in range(2)]
        m_new = [jnp.maximum(m_old[mi], jnp.max(s[mi], axis=0, keepdims=True)) for mi in range(2)]
        p = [jnp.exp2(s[mi] - m_new[mi]) for mi in range(2)]
        alpha = [jnp.exp2(m_old[mi] - m_new[mi]) for mi in range(2)]
        for mi in range(2):
            l_scr[mi] = alpha[mi] * l_scr[mi] + jnp.sum(p[mi], axis=0, keepdims=True)
            m_scr[mi] = m_new[mi]
        pv = [jnp.dot(vtc, p[mi].astype(BF16), preferred_element_type=F32) for mi in range(2)]
        for mi in range(2):
            acc_scr[mi] = alpha[mi] * acc_scr[mi] + pv[mi]

    scores(0, sa_scr)
    n_pairs = (nkv - 1) // 2

    def pair(i, carry):
        c = 2 * i
        scores(c + 1, sb_scr)
        accumulate(c, sa_scr)
        scores(c + 2, sa_scr)
        accumulate(c + 1, sb_scr)
        return carry

    lax.fori_loop(0, n_pairs, pair, 0)
    if nkv % 2 == 0:
        scores(nkv - 1, sb_scr)
        accumulate(nkv - 2, sa_scr)
        accumulate(nkv - 1, sb_scr)
    else:
        accumulate(nkv - 1, sa_scr)

    lp = lam_ref[...]
    lam = (jnp.exp(jnp.sum(lp[0:1] * lp[1:2], axis=-1, keepdims=True))
           - jnp.exp(jnp.sum(lp[2:3] * lp[3:4], axis=-1, keepdims=True)) + lam_init)
    o = (acc_scr[0] / l_scr[0] - lam * (acc_scr[1] / l_scr[1])).T
    ms = jnp.mean(o * o, axis=-1, keepdims=True)
    o_ref[...] = (o * lax.rsqrt(ms + SUBLN_EPS) * sw_ref[...] * (1.0 - lam_init)).astype(o_ref.dtype)


def _attention_call(lam_p, qt, k, vt, subln_w, *, q0, nq_rows, tq, kv0, nkv_rows, tk, lam_init, name):
    b, t, _ = k.shape
    hd = A_V_DIM
    assert q0 % tq == 0 and nq_rows % tq == 0 and kv0 % tk == 0 and nkv_rows % tk == 0
    qb0 = q0 // tq
    body = functools.partial(_attn_body, tq=tq, tk=tk, kv0=kv0, nkv=nkv_rows // tk, lam_init=lam_init)
    return pl.pallas_call(
        body,
        grid=(b, A_HEADS, nq_rows // tq),
        in_specs=[
            pl.BlockSpec((4, A_HEAD_DIM), lambda bi, h, i: (0, 0)),
            pl.BlockSpec((None, hd, tq), lambda bi, h, i: (bi, h, qb0 + i)),
            pl.BlockSpec((None, t, hd), lambda bi, h, i: (bi, 0, h)),
            pl.BlockSpec((None, hd, t), lambda bi, h, i: (bi, h, 0)),
            pl.BlockSpec((1, hd), lambda bi, h, i: (0, 0)),
        ],
        out_specs=pl.BlockSpec((None, tq, hd), lambda bi, h, i: (bi, i, h)),
        out_shape=jax.ShapeDtypeStruct((b, nq_rows, A_WIDTH), BF16),
        scratch_shapes=[pltpu.VMEM((2, 1, tq), F32), pltpu.VMEM((2, 1, tq), F32), pltpu.VMEM((2, hd, tq), F32),
                        pltpu.VMEM((2, tk, tq), F32), pltpu.VMEM((2, tk, tq), F32)],
        compiler_params=_cparams(("arbitrary", "arbitrary", "arbitrary")),
        name=name,
    )(lam_p, qt, k, vt, subln_w.reshape(1, hd))


def _diff_attention(lam_p, qt, k, vt, subln_w, s_len, lam_init):
    b, t, _ = k.shape
    ctx_len = t - s_len
    tq = _pick(s_len, (512, 256))
    tk = _pick(t, (768, 256))
    ya_lat = _attention_call(lam_p, qt, k, vt, subln_w, q0=0, nq_rows=s_len, tq=tq,
                             kv0=0, nkv_rows=t, tk=tk, lam_init=lam_init, name="diff_attention_lat")
    tc = _pick(ctx_len, (256, 128))
    assert s_len % tc == 0
    ya_ctx = _attention_call(lam_p, qt, k, vt, subln_w, q0=s_len, nq_rows=ctx_len, tq=tc,
                             kv0=s_len, nkv_rows=ctx_len, tk=tc, lam_init=lam_init, name="diff_attention_ctx")
    return ya_lat, ya_ctx


def _halo_specs(tm, t, width, col_blk):
    per = tm // HALO
    last = t // HALO - 1
    main = pl.BlockSpec((None, tm, width), lambda bi, i: (bi, i, col_blk))
    prev = pl.BlockSpec((None, HALO, width), lambda bi, i: (bi, jnp.maximum(i * per - 1, 0), col_blk))
    nxt = pl.BlockSpec((None, HALO, width), lambda bi, i: (bi, jnp.minimum((i + 1) * per, last), col_blk))
    return main, prev, nxt


def _halo_ok(i, s_tiles, n_tiles):
    prev_ok = jnp.logical_and(i != 0, i != s_tiles)
    next_ok = jnp.logical_and(i != s_tiles - 1, i != n_tiles - 1)
    return prev_ok, next_ok


def _conv_rows(buf, w_ref, r0, taps, c0, c1):
    base = HALO - taps // 2 + r0
    acc = jnp.zeros((ROW_CHUNK, c1 - c0), F32)
    for j in range(taps):
        acc = acc + buf[base + j:base + j + ROW_CHUNK, c0:c1] * w_ref[j:j + 1, c0:c1]
    return acc


def _conformer_body(a_ref, ap_ref, an_ref, g_ref, gp_ref, gn_ref, w_ref, b_ref, lw_ref, lb_ref, o_ref, buf,
                    *, tm, s_tiles, n_tiles):
    i = pl.program_id(1)
    prev_ok, next_ok = _halo_ok(i, s_tiles, n_tiles)

    def glu(a, g):
        return a[...].astype(F32) * jax.nn.sigmoid(g[...].astype(F32))

    buf[0:HALO, :] = jnp.where(prev_ok, glu(ap_ref, gp_ref), 0.0)
    buf[HALO:HALO + tm, :] = glu(a_ref, g_ref)
    buf[HALO + tm:2 * HALO + tm, :] = jnp.where(next_ok, glu(an_ref, gn_ref), 0.0)
    for r0 in range(0, tm, ROW_CHUNK):
        y = _conv_rows(buf, w_ref, r0, CONV_K, 0, CONV_CH) + b_ref[...]
        mu = jnp.mean(y, axis=-1, keepdims=True)
        yc = y - mu
        var = jnp.mean(yc * yc, axis=-1, keepdims=True)
        z = yc * lax.rsqrt(var + 1e-5) * lw_ref[...] + lb_ref[...]
        o_ref[r0:r0 + ROW_CHUNK, :] = _silu(z).astype(o_ref.dtype)


def _conformer(p, dw_w, dw_b, ln_w, ln_b, s_len):
    b, t, _ = p.shape
    tm = CONV_TM
    ch = CONV_CH
    a_specs = _halo_specs(tm, t, ch, PB_BA)
    g_specs = _halo_specs(tm, t, ch, PB_BG)
    vec = pl.BlockSpec((1, ch), lambda bi, i: (0, 0))
    body = functools.partial(_conformer_body, tm=tm, s_tiles=s_len // tm, n_tiles=t // tm)
    return pl.pallas_call(
        body,
        grid=(b, t // tm),
        in_specs=[*a_specs, *g_specs, pl.BlockSpec((CONV_K, ch), lambda bi, i: (0, 0)), vec, vec, vec],
        out_specs=pl.BlockSpec((None, tm, ch), lambda bi, i: (bi, i, 0)),
        out_shape=jax.ShapeDtypeStruct((b, t, ch), BF16),
        scratch_shapes=[pltpu.VMEM((tm + 2 * HALO, ch), F32)],
        compiler_params=_cparams(("arbitrary", "arbitrary")),
        name="conformer_conv",
    )(p, p, p, p, p, p, dw_w, dw_b.reshape(1, ch), ln_w.reshape(1, ch), ln_b.reshape(1, ch))


def _softplus(x):
    return jnp.maximum(x, 0.0) + jnp.log(1.0 + jnp.exp(-jnp.abs(x)))


def _gdnprep_body(x_ref, xp_ref, xn_ref, small_ref, w_ref, alog_ref, dtb_ref, q_ref, k_ref, v_ref, gb_ref, buf,
                  *, tm, s_tiles, n_tiles):
    i = pl.program_id(1)
    prev_ok, next_ok = _halo_ok(i, s_tiles, n_tiles)
    buf[0:HALO, :] = jnp.where(prev_ok, xp_ref[...].astype(F32), 0.0)
    buf[HALO:HALO + tm, :] = x_ref[...].astype(F32)
    buf[HALO + tm:2 * HALO + tm, :] = jnp.where(next_ok, xn_ref[...].astype(F32), 0.0)
    outs = (q_ref, k_ref, v_ref)
    for r0 in range(0, tm, ROW_CHUNK):
        for part in range(3):
            y = _silu(_conv_rows(buf, w_ref, r0, SHORT_K, part * C_WIDTH, (part + 1) * C_WIDTH))
            if part < 2:
                post = C_HEAD_DIM ** -0.5 if part == 0 else 1.0
                for h in range(C_HEADS):
                    yh = y[:, h * C_HEAD_DIM:(h + 1) * C_HEAD_DIM]
                    n = yh * lax.rsqrt(jnp.sum(yh * yh, axis=-1, keepdims=True) + 1e-6)
                    outs[part][r0:r0 + ROW_CHUNK, h * C_HEAD_DIM:(h + 1) * C_HEAD_DIM] = n * post
            else:
                outs[part][r0:r0 + ROW_CHUNK, :] = y
    sm = small_ref[...]
    lane = lax.broadcasted_iota(jnp.int32, sm.shape, 1)
    gdec = -jnp.exp(alog_ref[...]) * _softplus(sm + dtb_ref[...])
    gb_ref[...] = jnp.where(lane < 2 * C_HEADS, jax.nn.sigmoid(sm), gdec)


def _gdn_prep(p, small, conv_w, a_log, dt_bias, s_len):
    b, t, _ = p.shape
    tm = CONV_TM
    w3 = 3 * C_WIDTH
    x_specs = _halo_specs(tm, t, w3, PB_CQKV)
    pad = SMALL_W - 4 * C_HEADS
    alog = jnp.concatenate([jnp.zeros((2 * C_HEADS,), F32), a_log.reshape(-1), jnp.zeros((pad,), F32)]).reshape(1, SMALL_W)
    dtb = jnp.concatenate([jnp.zeros((2 * C_HEADS,), F32), dt_bias.reshape(-1), jnp.zeros((pad,), F32)]).reshape(1, SMALL_W)
    vec = pl.BlockSpec((1, SMALL_W), lambda bi, i: (0, 0))
    out_blk = pl.BlockSpec((None, tm, C_WIDTH), lambda bi, i: (bi, i, 0))
    body = functools.partial(_gdnprep_body, tm=tm, s_tiles=s_len // tm, n_tiles=t // tm)
    return pl.pallas_call(
        body,
        grid=(b, t // tm),
        in_specs=[*x_specs, pl.BlockSpec((None, tm, SMALL_W), lambda bi, i: (bi, i, 0)),
                  pl.BlockSpec((SHORT_K, w3), lambda bi, i: (0, 0)), vec, vec],
        out_specs=[out_blk, out_blk, out_blk, pl.BlockSpec((None, tm, SMALL_W), lambda bi, i: (bi, i, 0))],
        out_shape=[jax.ShapeDtypeStruct((b, t, C_WIDTH), F32)] * 3 + [jax.ShapeDtypeStruct((b, t, SMALL_W), F32)],
        scratch_shapes=[pltpu.VMEM((tm + 2 * HALO, w3), F32)],
        compiler_params=_cparams(("arbitrary", "arbitrary")),
        name="gdn_prep",
    )(p, p, p, small, conv_w, alog, dtb)


def _mm(a, b):
    return jnp.dot(a.astype(BF16), b.astype(BF16), preferred_element_type=F32)


def _mm_nt(a, b):
    return lax.dot_general(a.astype(BF16), b.astype(BF16), (((1,), (1,)), ((), ())), preferred_element_type=F32)


def _mm_tn(a, b):
    return lax.dot_general(a.astype(BF16), b.astype(BF16), (((0,), (0,)), ((), ())), preferred_element_type=F32)


def _gdn_chunks(chains):
    n = len(chains)
    c = chains[0][0].shape[0]
    ii = lax.broadcasted_iota(jnp.int32, (c, c), 0)
    jj = lax.broadcasted_iota(jnp.int32, (c, c), 1)
    eye = (ii == jj).astype(F32)
    incl = [(ii <= jj) if ch[7] else (ii >= jj) for ch in chains]
    strict = [(ii < jj) if ch[7] else (ii > jj) for ch in chains]
    q = [ch[0] for ch in chains]
    k = [ch[1] for ch in chains]
    v = [ch[2] for ch in chains]
    gcol = [ch[3] for ch in chains]
    bcol = [ch[5] for ch in chains]
    state = [ch[6] for ch in chains]
    decay = [jnp.where(incl[i], jnp.exp(jnp.where(incl[i], gcol[i] - chains[i][4], 0.0)), 0.0) for i in range(n)]
    kb = [k[i] * bcol[i] for i in range(n)]
    low = [jnp.where(strict[i], _mm_nt(kb[i], k[i]) * decay[i], 0.0) for i in range(n)]
    intra = [jnp.where(incl[i], _mm_nt(q[i], k[i]) * decay[i], 0.0) for i in range(n)]
    pw = [-low[i] for i in range(n)]
    t_inv = [eye + pw[i] for i in range(n)]
    for _ in range(int(math.log2(c)) - 1):
        pw = [_mm(pw[i], pw[i]) for i in range(n)]
        t_inv = [t_inv[i] + _mm(t_inv[i], pw[i]) for i in range(n)]
    eg = [jnp.exp(gcol[i]) for i in range(n)]
    u = [_mm(t_inv[i], v[i] * bcol[i]) for i in range(n)]
    w = [_mm(t_inv[i], kb[i] * eg[i]) for i in range(n)]
    g_last = [gcol[i][0:1, :] if chains[i][7] else gcol[i][c - 1:c, :] for i in range(n)]
    k_dec = [k[i] * jnp.exp(g_last[i] - gcol[i]) for i in range(n)]
    q_dec = [q[i] * eg[i] for i in range(n)]
    v_new = [u[i] - _mm(w[i], state[i]) for i in range(n)]
    o = [_mm(q_dec[i], state[i]) + _mm(intra[i], v_new[i]) for i in range(n)]
    new_state = [state[i] * jnp.exp(g_last[i]) + _mm_tn(k_dec[i], v_new[i]) for i in range(n)]
    return list(zip(o, new_state))


def _gdn_body(qf_ref, kf_ref, vf_ref, gf_ref, qb_ref, kb_ref, vb_ref, gbb_ref, of_ref, ob_ref, st_scr):
    @pl.when(pl.program_id(1) == 0)
    def _():
        st_scr[...] = jnp.zeros(st_scr.shape, F32)

    c = GDN_CHUNK
    ii = lax.broadcasted_iota(jnp.int32, (c, c), 0)
    jj = lax.broadcasted_iota(jnp.int32, (c, c), 1)
    dirs = ((qf_ref, kf_ref, vf_ref, gf_ref, of_ref, False), (qb_ref, kb_ref, vb_ref, gbb_ref, ob_ref, True))
    chains = []
    for d, (q_ref, k_ref, v_ref, g_ref, o_ref, rev) in enumerate(dirs):
        gb = g_ref[...]
        tri = ((ii <= jj) if rev else (ii >= jj)).astype(F32)
        gc = jnp.dot(tri, gb, preferred_element_type=F32, precision=HIGHEST)
        gct = gc.T
        for h in range(C_HEADS):
            sl = slice(h * C_HEAD_DIM, (h + 1) * C_HEAD_DIM)
            bc = d * C_HEADS + h
            gcn = 2 * C_HEADS + bc
            chains.append((q_ref[:, sl], k_ref[:, sl], v_ref[:, sl], gc[:, gcn:gcn + 1], gct[gcn:gcn + 1, :],
                           gb[:, bc:bc + 1], st_scr[bc], rev))
    results = _gdn_chunks(chains)
    for d, (_, _, _, _, o_ref, _) in enumerate(dirs):
        for h in range(C_HEADS):
            bc = d * C_HEADS + h
            o, st = results[bc]
            o_ref[:, h * C_HEAD_DIM:(h + 1) * C_HEAD_DIM] = o
            st_scr[bc] = st


def _gdn_scan(q, k, v, gb, s_len):
    b, t, w = q.shape
    c = GDN_CHUNK
    n_lat = s_len // c
    n_all = t // c
    n_ctx = n_all - n_lat

    def fwd(bi, s):
        return (bi, jnp.where(s < n_ctx, n_lat + s, s - n_ctx), 0)

    def bwd(bi, s):
        return (bi, jnp.where(s < n_ctx, n_all - 1 - s, n_all - 1 - s), 0)

    wide_f = pl.BlockSpec((None, c, w), fwd)
    wide_b = pl.BlockSpec((None, c, w), bwd)
    nar_f = pl.BlockSpec((None, c, SMALL_W), fwd)
    nar_b = pl.BlockSpec((None, c, SMALL_W), bwd)
    return pl.pallas_call(
        _gdn_body,
        grid=(b, n_all),
        in_specs=[wide_f, wide_f, wide_f, nar_f, wide_b, wide_b, wide_b, nar_b],
        out_specs=[wide_f, wide_b],
        out_shape=[jax.ShapeDtypeStruct((b, t, w), F32)] * 2,
        scratch_shapes=[pltpu.VMEM((2 * C_HEADS, C_HEAD_DIM, C_HEAD_DIM), F32)],
        compiler_params=_cparams(("arbitrary", "arbitrary")),
        name="gdn_scan",
    )(q, k, v, gb, q, k, v, gb)


def _merge_body(x_ref, yal_ref, yac_ref, yb_ref, of_ref, ob_ref, z_ref, g0_ref, g1_ref, g2_ref, gnw_ref, wb_ref,
                wo_ref, gate_ref, n2_ref, sh_ref, sc_ref, rw_ref, xo_ref, h2_ref, sco_ref, *, tm, s_len):
    i = pl.program_id(1)
    is_ctx = _rows_are_ctx(i, tm, s_len)
    ya = jnp.where(i * tm >= s_len, yac_ref[...], yal_ref[...])
    o = of_ref[...] + ob_ref[...]
    z = z_ref[...].astype(F32)
    parts = []
    for h in range(C_HEADS):
        sl = slice(h * C_HEAD_DIM, (h + 1) * C_HEAD_DIM)
        oh = o[:, sl]
        ms = jnp.mean(oh * oh, axis=-1, keepdims=True)
        parts.append((oh * lax.rsqrt(ms + EPS) * gnw_ref[...] * _silu(z[:, sl])).astype(BF16))
    yc = jnp.concatenate(parts, axis=-1)
    m = jnp.zeros((tm, D_MODEL), F32)
    for bi, (y, g_ref) in enumerate(((ya, g0_ref), (yb_ref[...], g1_ref), (yc, g2_ref))):
        m = m + jax.nn.sigmoid(g_ref[...].astype(F32)) * jnp.dot(y, wb_ref[bi], preferred_element_type=F32)
    upd = jnp.dot(m.astype(BF16), wo_ref[...], preferred_element_type=F32)
    x = x_ref[...] + _pick_mod(gate_ref, is_ctx) * upd
    xo_ref[...] = x
    h2 = _rms_modulate(x, n2_ref[...], _pick_mod(sh_ref, is_ctx), _pick_mod(sc_ref, is_ctx))
    h2_ref[...] = h2
    sco_ref[...] = jax.nn.sigmoid(jnp.dot(h2, rw_ref[...], preferred_element_type=F32, precision=HIGHEST))


def _merge(xs, ya_lat, ya_ctx, yb, o_f, o_b, p, gdn_norm_w, w_branch, w_out, modb, norm2_w, router_w, s_len):
    b, t, d = xs.shape
    tm = CONV_TM
    bw = BRANCH_W
    e_pad = LANES
    s_tiles = s_len // tm
    n_tiles = t // tm
    lat_spec = pl.BlockSpec((None, tm, bw), lambda bi, i: (bi, jnp.minimum(i, s_tiles - 1), 0))
    ctx_spec = pl.BlockSpec((None, tm, bw), lambda bi, i: (bi, jnp.maximum(i - s_tiles, 0), 0))
    rw = jnp.zeros((d, e_pad), F32).at[:, :N_EXPERTS].set(router_w)
    tok = lambda width, blk: pl.BlockSpec((None, tm, width), lambda bi, i: (bi, i, blk))
    modspec = lambda k: pl.BlockSpec((None, 2, d), lambda bi, i: (bi, 0, k))
    full = lambda shape: pl.BlockSpec(shape, lambda bi, i: (0,) * len(shape))
    body = functools.partial(_merge_body, tm=tm, s_len=s_len)
    return pl.pallas_call(
        body,
        grid=(b, t // tm),
        in_specs=[
            tok(d, 0), lat_spec, ctx_spec, tok(bw, 0), tok(bw, 0), tok(bw, 0), tok(bw, PB_CZ),
            tok(d, 0), tok(d, 1), tok(d, 2),
            full((1, C_HEAD_DIM)), full((N_BRANCH, bw, d)), full((d, d)),
            modspec(2), full((1, d)), modspec(3), modspec(4), full((d, e_pad)),
        ],
        out_specs=[tok(d, 0), tok(d, 0), tok(e_pad, 0)],
        out_shape=[jax.ShapeDtypeStruct((b, t, d), F32), jax.ShapeDtypeStruct((b, t, d), F32),
                   jax.ShapeDtypeStruct((b, t, e_pad), F32)],
        compiler_params=_cparams(("arbitrary", "arbitrary")),
        name="merge_norm_router",
    )(xs, ya_lat, ya_ctx, yb, o_f, o_b, p, p, p, p, gdn_norm_w.reshape(1, C_HEAD_DIM), w_branch, w_out,
      modb, norm2_w.reshape(1, d), modb, modb, rw)


MOE_ROWS = 256


def _routing_tables(scores, router_bias):
    n = scores.shape[0]
    rb = MOE_ROWS
    grp = (scores + router_bias.astype(F32)).reshape(n, N_GROUPS, EXPERTS_PER_GROUP)
    best = jnp.argmax(lax.top_k(grp, TOP_K)[0].sum(-1), axis=-1)
    in_grp = jnp.take_along_axis(grp, best[:, None, None], axis=1)[:, 0]
    _, local = lax.top_k(in_grp, TOP_K)
    expert = best[:, None] * EXPERTS_PER_GROUP + local
    w = jnp.take_along_axis(scores, expert, axis=-1)
    gate = w / jnp.sum(w, axis=-1, keepdims=True)
    lo = jnp.min(local, axis=-1)
    hi = jnp.max(local, axis=-1)
    first_is_lo = local[:, 0] < local[:, 1]
    g_lo = jnp.where(first_is_lo, gate[:, 0], gate[:, 1])
    g_hi = jnp.where(first_is_lo, gate[:, 1], gate[:, 0])
    pair = lo * (2 * EXPERTS_PER_GROUP - lo - 1) // 2 + (hi - lo - 1)
    cls = (best * N_PAIR + pair).astype(jnp.int32)

    blk = LANES
    onehot = (cls[:, None] == jnp.arange(N_CLASS, dtype=jnp.int32)[None, :]).astype(F32).reshape(n // blk, blk, N_CLASS)
    tril = (jnp.arange(blk)[:, None] >= jnp.arange(blk)[None, :]).astype(F32)
    within = jnp.einsum("ij,bjc->bic", tril, onehot)
    blk_tot = within[:, -1, :]
    blk_off = jnp.cumsum(blk_tot, axis=0) - blk_tot
    counts = (blk_off[-1] + blk_tot[-1]).astype(jnp.int32)
    rank = jnp.sum(onehot * (within - 1.0 + blk_off[:, None, :]), axis=-1).reshape(n).astype(jnp.int32)
    padded = (counts + rb - 1) // rb * rb
    pad_end = jnp.cumsum(padded)
    pad_start = (pad_end - padded).astype(F32)
    dest = (jnp.sum(onehot * pad_start[None, None, :], axis=-1).reshape(n).astype(jnp.int32) + rank)
    n_blocks = n // rb + N_CLASS
    n_rows = n_blocks * rb
    row_tok = jnp.zeros((n_rows,), jnp.int32).at[dest].set(jnp.arange(n, dtype=jnp.int32))
    tok_row = dest
    row_gate = jnp.zeros((n_rows, LANES), F32).at[dest, 0].set(g_lo).at[dest, 1].set(g_hi)
    n_used = (pad_end[-1] // rb).astype(jnp.int32)
    blk = jnp.minimum(jnp.arange(n_blocks, dtype=jnp.int32), n_used - 1)
    blk_cls = jnp.minimum(jnp.searchsorted(pad_end, blk * rb, side="right"), N_CLASS - 1).astype(jnp.int32)
    pl_lo, pl_hi = [], []
    for a in range(EXPERTS_PER_GROUP):
        for c in range(a + 1, EXPERTS_PER_GROUP):
            pl_lo.append(a)
            pl_hi.append(c)
    pl_lo = jnp.asarray(pl_lo, jnp.int32)
    pl_hi = jnp.asarray(pl_hi, jnp.int32)
    blk_grp = blk_cls // N_PAIR
    blk_pair = blk_cls % N_PAIR
    e_lo = blk_grp * EXPERTS_PER_GROUP + pl_lo[blk_pair]
    e_hi = blk_grp * EXPERTS_PER_GROUP + pl_hi[blk_pair]
    return e_lo, e_hi, n_used.reshape(1), row_tok, tok_row, row_gate


def _gather_rows(idx_ref, base, n_rows, src_hbm, dst, sem):
    def issue(r, carry):
        pltpu.make_async_copy(src_hbm.at[pl.ds(idx_ref[base + r], 1), :], dst.at[pl.ds(r, 1), :], sem).start()
        return carry

    lax.fori_loop(0, n_rows, issue, 0)
    pltpu.make_async_copy(src_hbm.at[pl.ds(0, n_rows), :], dst, sem).wait()


def _moe_body(elo_ref, ehi_ref, nused_ref, rowtok_ref, h2_hbm, gate_ref, wg_lo, wu_lo, wd_lo, wg_hi, wu_hi, wd_hi,
              y_ref, xbuf, sem, *, rb):
    i = pl.program_id(0)

    @pl.when(i < nused_ref[0])
    def _():
        _gather_rows(rowtok_ref, i * rb, rb, h2_hbm, xbuf, sem)
        x = xbuf[...].astype(BF16)
        gate = gate_ref[...]

        def ffn(wg, wu, wd):
            hg = jnp.dot(x, wg[...], preferred_element_type=F32)
            hu = jnp.dot(x, wu[...], preferred_element_type=F32)
            return jnp.dot((_silu(hg) * hu).astype(BF16), wd[...], preferred_element_type=F32)

        y_ref[...] = gate[:, 0:1] * ffn(wg_lo, wu_lo, wd_lo) + gate[:, 1:2] * ffn(wg_hi, wu_hi, wd_hi)

    @pl.when(i >= nused_ref[0])
    def _():
        y_ref[...] = jnp.zeros(y_ref.shape, F32)


def _moe_ffn(h2_flat, tables, w_gate, w_up, w_down):
    e_lo, e_hi, n_used, row_tok, _, row_gate = tables
    n, d = h2_flat.shape
    rb = MOE_ROWS
    n_blocks = e_lo.shape[0]
    de = w_gate.shape[-1]
    w_in_lo = pl.BlockSpec((None, d, de), lambda i, lo, hi, nu, rt: (lo[i], 0, 0))
    w_in_hi = pl.BlockSpec((None, d, de), lambda i, lo, hi, nu, rt: (hi[i], 0, 0))
    w_dn_lo = pl.BlockSpec((None, de, d), lambda i, lo, hi, nu, rt: (lo[i], 0, 0))
    w_dn_hi = pl.BlockSpec((None, de, d), lambda i, lo, hi, nu, rt: (hi[i], 0, 0))
    grid_spec = pltpu.PrefetchScalarGridSpec(
        num_scalar_prefetch=4,
        grid=(n_blocks,),
        in_specs=[
            pl.BlockSpec(memory_space=pl.ANY),
            pl.BlockSpec((rb, LANES), lambda i, lo, hi, nu, rt: (i, 0)),
            w_in_lo, w_in_lo, w_dn_lo, w_in_hi, w_in_hi, w_dn_hi,
        ],
        out_specs=pl.BlockSpec((rb, d), lambda i, lo, hi, nu, rt: (i, 0)),
        scratch_shapes=[pltpu.VMEM((rb, d), F32), pltpu.SemaphoreType.DMA(())],
    )
    return pl.pallas_call(
        functools.partial(_moe_body, rb=rb),
        grid_spec=grid_spec,
        out_shape=jax.ShapeDtypeStruct((n_blocks * rb, d), F32),
        compiler_params=_cparams(("arbitrary",)),
        name="moe_expert_ffn",
    )(e_lo, e_hi, n_used, row_tok, h2_flat, row_gate, w_gate, w_up, w_down, w_gate, w_up, w_down)


def _combine_body(tokrow_ref, x_ref, gate_ref, y_hbm, o_ref, ybuf, sem, *, tm, s_len, tiles_per_batch):
    bi = pl.program_id(0)
    i = pl.program_id(1)
    _gather_rows(tokrow_ref, (bi * tiles_per_batch + i) * tm, tm, y_hbm, ybuf, sem)
    is_ctx = _rows_are_ctx(i, tm, s_len)
    o_ref[...] = x_ref[...] + _pick_mod(gate_ref, is_ctx) * ybuf[...]


def _combine(xs, y_sorted, tok_row, modb, s_len):
    b, t, d = xs.shape
    tm = _pick(t, (256,))
    grid_spec = pltpu.PrefetchScalarGridSpec(
        num_scalar_prefetch=1,
        grid=(b, t // tm),
        in_specs=[
            pl.BlockSpec((None, tm, d), lambda bi, i, tr: (bi, i, 0)),
            pl.BlockSpec((None, 2, d), lambda bi, i, tr: (bi, 0, 5)),
            pl.BlockSpec(memory_space=pl.ANY),
        ],
        out_specs=pl.BlockSpec((None, tm, d), lambda bi, i, tr: (bi, i, 0)),
        scratch_shapes=[pltpu.VMEM((tm, d), F32), pltpu.SemaphoreType.DMA(())],
    )
    body = functools.partial(_combine_body, tm=tm, s_len=s_len, tiles_per_batch=t // tm)
    return pl.pallas_call(
        body,
        grid_spec=grid_spec,
        out_shape=jax.ShapeDtypeStruct((b, t, d), F32),
        compiler_params=_cparams(("arbitrary", "arbitrary")),
        name="moe_combine_residual",
    )(tok_row, xs, modb, y_sorted)


def _rope_tables(s_len, ctx_len):
    t = jnp.arange(s_len, dtype=jnp.int32)
    r = (t // GRID_W).astype(F32)
    col = (t % GRID_W).astype(F32)
    axis_dim = A_HEAD_DIM // 2
    inv_freq = ROPE_THETA ** (-jnp.arange(0, axis_dim, 2, dtype=F32) / axis_dim)
    ang_r = r[:, None] * inv_freq
    ang_c = col[:, None] * inv_freq
    cos = jnp.concatenate([jnp.cos(ang_r), jnp.cos(ang_r), jnp.cos(ang_c), jnp.cos(ang_c)], axis=-1)
    sin = jnp.concatenate([-jnp.sin(ang_r), jnp.sin(ang_r), -jnp.sin(ang_c), jnp.sin(ang_c)], axis=-1)
    reps = A_WIDTH // A_HEAD_DIM
    cos = jnp.concatenate([jnp.tile(cos, (1, reps)), jnp.ones((ctx_len, A_WIDTH), F32)], axis=0)
    sin = jnp.concatenate([jnp.tile(sin, (1, reps)), jnp.zeros((ctx_len, A_WIDTH), F32)], axis=0)
    return cos, sin


def _split_w_in(w_in_l):
    oa, oc, ob = A_COLS, A_COLS + C_COLS, A_COLS + C_COLS + B_COLS
    a = w_in_l[:, :oa]
    c_main = w_in_l[:, oa:oa + 4 * C_WIDTH]
    c_small = w_in_l[:, oa + 4 * C_WIDTH:oc]
    bb = w_in_l[:, oc:ob]
    gates = w_in_l[:, ob:]
    main = jnp.concatenate([gates, a, c_main, bb], axis=1).astype(BF16)
    small = jnp.zeros((w_in_l.shape[0], SMALL_W), F32).at[:, :4 * C_HEADS].set(c_small).astype(BF16)
    return main, small


def kernel(x, c, ctx, c_ctx, w_mod, b_mod, norm1_w, norm2_w, w_in, qn_w, kn_w, lam_p, subln_w, conv_dw_w, conv_dw_b,
           conv_ln_w, conv_ln_b, gdn_conv_w, gdn_a_log, gdn_dt_bias, gdn_norm_w, w_branch, w_out, router_w,
           router_bias, w_gate, w_up, w_down):
    b, s_len, d = x.shape
    ctx_len = ctx.shape[1]
    depth = w_mod.shape[0]
    t = s_len + ctx_len
    assert s_len % CONV_TM == 0 and ctx_len % CONV_TM == 0 and s_len % GRID_W == 0

    xs = jnp.concatenate([x, ctx], axis=1)
    rows = -(-(b + 1) // 8) * 8
    cvec = jnp.zeros((rows, d), F32).at[:b].set(c).at[b].set(c_ctx)
    mod = _modulation(cvec, w_mod, b_mod)
    rope_c, rope_s = _rope_tables(s_len, ctx_len)

    for l in range(depth):
        lam_init = 0.8 - 0.6 * math.exp(-0.3 * l)
        modb = jnp.stack([mod[l, :b], jnp.broadcast_to(mod[l, b], (b, N_MOD * d))], axis=1)
        w_main, w_small = _split_w_in(w_in[l])

        p, small = _in_projection(xs, norm1_w[l], modb, w_main, w_small, s_len)
        qt, k, vt = _attn_prep(p, rope_c, rope_s, qn_w[l], kn_w[l])
        ya_lat, ya_ctx = _diff_attention(lam_p[l], qt, k, vt, subln_w[l], s_len, lam_init)
        yb = _conformer(p, conv_dw_w[l], conv_dw_b[l], conv_ln_w[l], conv_ln_b[l], s_len)
        gq, gk, gv, gb = _gdn_prep(p, small, gdn_conv_w[l], gdn_a_log[l], gdn_dt_bias[l], s_len)
        o_f, o_b = _gdn_scan(gq, gk, gv, gb, s_len)
        xs, h2, scores = _merge(xs, ya_lat, ya_ctx, yb, o_f, o_b, p, gdn_norm_w[l], w_branch[l].astype(BF16),
                                w_out[l].astype(BF16), modb, norm2_w[l], router_w, s_len)

        tables = _routing_tables(scores.reshape(b * t, -1)[:, :N_EXPERTS], router_bias)
        y_sorted = _moe_ffn(h2.reshape(b * t, d), tables, w_gate[l].astype(BF16), w_up[l].astype(BF16),
                            w_down[l].astype(BF16))
        xs = _combine(xs, y_sorted, tables[4], modb, s_len)
    return xs[:, :s_len]
```

```python
import functools
import math

import jax
import jax.numpy as jnp
from jax import lax
from jax.experimental import pallas as pl
from jax.experimental.pallas import tpu as pltpu

F32 = jnp.float32
BF16 = jnp.bfloat16
HIGHEST = lax.Precision.HIGHEST

D_MODEL = 1024
GRID_W = 64
N_MOD = 6
EPS = 1e-6

A_HEADS = 4
A_HEAD_DIM = 64
A_V_DIM = 2 * A_HEAD_DIM
A_WIDTH = A_HEADS * A_V_DIM
ROPE_THETA = 10000.0
SUBLN_EPS = 1e-5

CONV_CH = 512
CONV_K = 31

C_HEADS = 4
C_HEAD_DIM = 128
C_WIDTH = C_HEADS * C_HEAD_DIM
SHORT_K = 5
GDN_CHUNK = 64

N_BRANCH = 3
BRANCH_W = 512

A_COLS = 3 * A_WIDTH
C_COLS = 4 * C_WIDTH + 4 * C_HEADS
B_COLS = 2 * CONV_CH
GATE_COLS = N_BRANCH * D_MODEL

N_EXPERTS = 16
N_GROUPS = 4
EXPERTS_PER_GROUP = N_EXPERTS // N_GROUPS
TOP_K = 2
D_EXPERT = 1024
N_PAIR = EXPERTS_PER_GROUP * (EXPERTS_PER_GROUP - 1) // 2
N_CLASS = N_GROUPS * N_PAIR

LANES = 128
HALO = 16
CONV_TM = 256
ROW_CHUNK = 32
VMEM_LIMIT = 52 * 1024 * 1024

P_COLS = GATE_COLS + A_COLS + 4 * C_WIDTH + B_COLS
PB_AQ, PB_AK, PB_AV = 6, 7, 8
PB_CQKV = 3
PB_CZ, PB_BA, PB_BG = 12, 13, 14
SMALL_W = LANES


def _cparams(sem):
    return pltpu.CompilerParams(dimension_semantics=sem, vmem_limit_bytes=VMEM_LIMIT)


def _silu(x):
    return x * jax.nn.sigmoid(x)


def _pick(n, prefs):
    for p in prefs:
        if n % p == 0:
            return p
    raise ValueError(f"no tile in {prefs} divides {n}")


def _mod_body(cv_ref, w_ref, b_ref, o_ref):
    s = _silu(cv_ref[...])
    o_ref[...] = jnp.dot(s, w_ref[...], preferred_element_type=F32, precision=HIGHEST) + b_ref[...]


def _modulation(cvec, w_mod, b_mod):
    depth, d, n = w_mod.shape
    r = cvec.shape[0]
    tn = _pick(n, (1536, 1024, 512, 128))
    return pl.pallas_call(
        _mod_body,
        grid=(depth, n // tn),
        in_specs=[
            pl.BlockSpec((r, d), lambda l, j: (0, 0)),
            pl.BlockSpec((None, d, tn), lambda l, j: (l, 0, j)),
            pl.BlockSpec((None, 1, tn), lambda l, j: (l, 0, j)),
        ],
        out_specs=pl.BlockSpec((None, r, tn), lambda l, j: (l, 0, j)),
        out_shape=jax.ShapeDtypeStruct((depth, r, n), F32),
        compiler_params=_cparams(("arbitrary", "arbitrary")),
        name="modulation",
    )(cvec, w_mod, b_mod.reshape(depth, 1, n))


def _rows_are_ctx(tile_idx, tm, s_len):
    rows = tile_idx * tm + lax.broadcasted_iota(jnp.int32, (tm, 1), 0)
    return rows >= s_len


def _pick_mod(mod_ref, is_ctx):
    return jnp.where(is_ctx, mod_ref[1:2, :], mod_ref[0:1, :])


def _rms_modulate(x, w, shift, scale):
    ms = jnp.mean(x * x, axis=-1, keepdims=True)
    return x * lax.rsqrt(ms + EPS) * w * (1.0 + scale) + shift


def _inproj_body(x_ref, nw_ref, sh_ref, sc_ref, w_ref, ws_ref, p_ref, small_ref, h_scr, *, tm, s_len):
    i = pl.program_id(1)
    j = pl.program_id(2)

    @pl.when(j == 0)
    def _():
        is_ctx = _rows_are_ctx(i, tm, s_len)
        h = _rms_modulate(x_ref[...], nw_ref[...], _pick_mod(sh_ref, is_ctx), _pick_mod(sc_ref, is_ctx))
        h = h.astype(BF16)
        h_scr[...] = h
        small_ref[...] = jnp.dot(h, ws_ref[...], preferred_element_type=F32)

    p_ref[...] = jnp.dot(h_scr[...], w_ref[...], preferred_element_type=F32).astype(BF16)


def _in_projection(xs, norm_w, modb, w_main, w_small, s_len):
    b, t, d = xs.shape
    n = w_main.shape[1]
    tm = _pick(t, (768, 512, 256))
    tn = _pick(n, (1536, 512))
    body = functools.partial(_inproj_body, tm=tm, s_len=s_len)
    return pl.pallas_call(
        body,
        grid=(b, t // tm, n // tn),
        in_specs=[
            pl.BlockSpec((None, tm, d), lambda bi, i, j: (bi, i, 0)),
            pl.BlockSpec((1, d), lambda bi, i, j: (0, 0)),
            pl.BlockSpec((None, 2, d), lambda bi, i, j: (bi, 0, 0)),
            pl.BlockSpec((None, 2, d), lambda bi, i, j: (bi, 0, 1)),
            pl.BlockSpec((d, tn), lambda bi, i, j: (0, j)),
            pl.BlockSpec((d, SMALL_W), lambda bi, i, j: (0, 0)),
        ],
        out_specs=[
            pl.BlockSpec((None, tm, tn), lambda bi, i, j: (bi, i, j)),
            pl.BlockSpec((None, tm, SMALL_W), lambda bi, i, j: (bi, i, 0)),
        ],
        out_shape=[
            jax.ShapeDtypeStruct((b, t, n), BF16),
            jax.ShapeDtypeStruct((b, t, SMALL_W), F32),
        ],
        scratch_shapes=[pltpu.VMEM((tm, d), BF16)],
        compiler_params=_cparams(("arbitrary", "arbitrary", "arbitrary")),
        name="in_projection",
    )(xs, norm_w.reshape(1, d), modb, modb, w_main, w_small)


def _group_mean(x2, bd):
    hi = x2.astype(BF16)
    lo = (x2 - hi.astype(F32)).astype(BF16)
    return (jnp.dot(hi, bd, preferred_element_type=F32) + jnp.dot(lo, bd, preferred_element_type=F32))


def _rope_partner(y):
    n = y.shape[-1]
    lane = lax.broadcasted_iota(jnp.int32, y.shape, 1)
    return jnp.where((lane & 16) == 0, pltpu.roll(y, n - 16, 1), pltpu.roll(y, 16, 1))


def _attnprep_body(q_ref, k_ref, v_ref, rc_ref, rs_ref, qw_ref, kw_ref, bd_ref, qt_ref, ko_ref, vt_ref):
    bd = bd_ref[...]
    rc = rc_ref[...]
    rs = rs_ref[...]

    def prep(x_ref, w_ref, post):
        x = x_ref[...].astype(F32)
        y = x * lax.rsqrt(_group_mean(x * x, bd) + EPS) * w_ref[...]
        y = y * rc + _rope_partner(y) * rs
        return y * post

    qt_ref[...] = prep(q_ref, qw_ref, A_HEAD_DIM ** -0.5 * math.log2(math.e)).T.astype(BF16)
    ko_ref[...] = prep(k_ref, kw_ref, 1.0).astype(BF16)
    vt_ref[...] = v_ref[...].astype(F32).T.astype(BF16)


def _attn_prep(p, rope_c, rope_s, qn_w, kn_w):
    b, t, _ = p.shape
    w = A_WIDTH
    tm = _pick(t, (768, 512, 256))
    g = jnp.arange(w) // A_HEAD_DIM
    bd = jnp.where(g[:, None] == g[None, :], 1.0 / A_HEAD_DIM, 0.0).astype(BF16)
    reps = w // A_HEAD_DIM
    tok = lambda blk: pl.BlockSpec((None, tm, w), lambda bi, i: (bi, i, blk))
    tr = pl.BlockSpec((None, w, tm), lambda bi, i: (bi, 0, i))
    return pl.pallas_call(
        _attnprep_body,
        grid=(b, t // tm),
        in_specs=[
            tok(PB_AQ), tok(PB_AK), tok(PB_AV),
            pl.BlockSpec((tm, w), lambda bi, i: (i, 0)),
            pl.BlockSpec((tm, w), lambda bi, i: (i, 0)),
            pl.BlockSpec((1, w), lambda bi, i: (0, 0)),
            pl.BlockSpec((1, w), lambda bi, i: (0, 0)),
            pl.BlockSpec((w, w), lambda bi, i: (0, 0)),
        ],
        out_specs=[tr, tok(0), tr],
        out_shape=[jax.ShapeDtypeStruct((b, w, t), BF16), jax.ShapeDtypeStruct((b, t, w), BF16),
                   jax.ShapeDtypeStruct((b, w, t), BF16)],
        compiler_params=_cparams(("arbitrary", "arbitrary")),
        name="attn_prep",
    )(p, p, p, rope_c, rope_s, jnp.tile(qn_w, reps).reshape(1, w), jnp.tile(kn_w, reps).reshape(1, w), bd)


def _attn_body(lam_ref, qt_ref, k_ref, vt_ref, sw_ref, o_ref, m_scr, l_scr, acc_scr, sa_scr, sb_scr,
               *, tq, tk, kv0, nkv, lam_init):
    qt = qt_ref[...]
    row = lax.broadcasted_iota(jnp.int32, qt.shape, 0)
    zero = jnp.zeros_like(qt)
    qm = (jnp.where(row < A_HEAD_DIM, qt, zero), jnp.where(row >= A_HEAD_DIM, qt, zero))
    m_scr[...] = jnp.full(m_scr.shape, -jnp.inf, F32)
    l_scr[...] = jnp.zeros(l_scr.shape, F32)
    acc_scr[...] = jnp.zeros(acc_scr.shape, F32)

    def scores(c, dst):
        off = pl.multiple_of(kv0 + c * tk, tk)
        kc = k_ref[pl.ds(off, tk), :]
        for mi in range(2):
            dst[mi] = jnp.dot(kc, qm[mi], preferred_element_type=F32)

    def accumulate(c, src):
        off = pl.multiple_of(kv0 + c * tk, tk)
        vtc = vt_ref[:, pl.ds(off, tk)]
        s = [src[mi] for mi in range(2)]
        m_old = [m_scr[mi] for mi in range(2)]
        m_new = [jnp.maximum(m_old[mi], jnp.max(s[mi], axis=0, keepdims=True)) for mi in range(2)]
        p = [jnp.exp2(s[mi] - m_new[mi]) for mi in range(2)]
        alpha = [jnp.exp2(m_old[mi] - m_new[mi]) for mi in range(2)]
        for mi in range(2):
            l_scr[mi] = alpha[mi] * l_scr[mi] + jnp.sum(p[mi], axis=0, keepdims=True)
            m_scr[mi] = m_new[mi]
        pv = [jnp.dot(vtc, p[mi].astype(BF16), preferred_element_type=F32) for mi in range(2)]
        for mi in range(2):
            acc_scr[mi] = alpha[mi] * acc_scr[mi] + pv[mi]

    scores(0, sa_scr)
    n_pairs = (nkv - 1) // 2

    def pair(i, carry):
        c = 2 * i
        scores(c + 1, sb_scr)
        accumulate(c, sa_scr)
        scores(c + 2, sa_scr)
        accumulate(c + 1, sb_scr)
        return carry

    lax.fori_loop(0, n_pairs, pair, 0)
    if nkv % 2 == 0:
        scores(nkv - 1, sb_scr)
        accumulate(nkv - 2, sa_scr)
        accumulate(nkv - 1, sb_scr)
    else:
        accumulate(nkv - 1, sa_scr)

    lp = lam_ref[...]
    lam = (jnp.exp(jnp.sum(lp[0:1] * lp[1:2], axis=-1, keepdims=True))
           - jnp.exp(jnp.sum(lp[2:3] * lp[3:4], axis=-1, keepdims=True)) + lam_init)
    o = (acc_scr[0] / l_scr[0] - lam * (acc_scr[1] / l_scr[1])).T
    ms = jnp.mean(o * o, axis=-1, keepdims=True)
    o_ref[...] = (o * lax.rsqrt(ms + SUBLN_EPS) * sw_ref[...] * (1.0 - lam_init)).astype(o_ref.dtype)


def _attention_call(lam_p, qt, k, vt, subln_w, *, q0, nq_rows, tq, kv0, nkv_rows, tk, lam_init, name):
    b, t, _ = k.shape
    hd = A_V_DIM
    assert q0 % tq == 0 and nq_rows % tq == 0 and kv0 % tk == 0 and nkv_rows % tk == 0
    qb0 = q0 // tq
    body = functools.partial(_attn_body, tq=tq, tk=tk, kv0=kv0, nkv=nkv_rows // tk, lam_init=lam_init)
    return pl.pallas_call(
        body,
        grid=(b, A_HEADS, nq_rows // tq),
        in_specs=[
            pl.BlockSpec((4, A_HEAD_DIM), lambda bi, h, i: (0, 0)),
            pl.BlockSpec((None, hd, tq), lambda bi, h, i: (bi, h, qb0 + i)),
            pl.BlockSpec((None, t, hd), lambda bi, h, i: (bi, 0, h)),
            pl.BlockSpec((None, hd, t), lambda bi, h, i: (bi, h, 0)),
            pl.BlockSpec((1, hd), lambda bi, h, i: (0, 0)),
        ],
        out_specs=pl.BlockSpec((None, tq, hd), lambda bi, h, i: (bi, i, h)),
        out_shape=jax.ShapeDtypeStruct((b, nq_rows, A_WIDTH), BF16),
        scratch_shapes=[pltpu.VMEM((2, 1, tq), F32), pltpu.VMEM((2, 1, tq), F32), pltpu.VMEM((2, hd, tq), F32),
                        pltpu.VMEM((2, tk, tq), F32), pltpu.VMEM((2, tk, tq), F32)],
        compiler_params=_cparams(("arbitrary", "arbitrary", "arbitrary")),
        name=name,
    )(lam_p, qt, k, vt, subln_w.reshape(1, hd))


def _diff_attention(lam_p, qt, k, vt, subln_w, s_len, lam_init):
    b, t, _ = k.shape
    ctx_len = t - s_len
    tq = _pick(s_len, (512, 256))
    tk = _pick(t, (1408, 768, 256))
    ya_lat = _attention_call(lam_p, qt, k, vt, subln_w, q0=0, nq_rows=s_len, tq=tq,
                             kv0=0, nkv_rows=t, tk=tk, lam_init=lam_init, name="diff_attention_lat")
    tc = _pick(ctx_len, (256, 128))
    assert s_len % tc == 0
    ya_ctx = _attention_call(lam_p, qt, k, vt, subln_w, q0=s_len, nq_rows=ctx_len, tq=tc,
                             kv0=s_len, nkv_rows=ctx_len, tk=tc, lam_init=lam_init, name="diff_attention_ctx")
    return ya_lat, ya_ctx


def _halo_specs(tm, t, width, col_blk):
    per = tm // HALO
    last = t // HALO - 1
    main = pl.BlockSpec((None, tm, width), lambda bi, i: (bi, i, col_blk))
    prev = pl.BlockSpec((None, HALO, width), lambda bi, i: (bi, jnp.maximum(i * per - 1, 0), col_blk))
    nxt = pl.BlockSpec((None, HALO, width), lambda bi, i: (bi, jnp.minimum((i + 1) * per, last), col_blk))
    return main, prev, nxt


def _halo_ok(i, s_tiles, n_tiles):
    prev_ok = jnp.logical_and(i != 0, i != s_tiles)
    next_ok = jnp.logical_and(i != s_tiles - 1, i != n_tiles - 1)
    return prev_ok, next_ok


def _conv_rows(buf, w_ref, r0, taps, c0, c1):
    base = HALO - taps // 2 + r0
    acc = jnp.zeros((ROW_CHUNK, c1 - c0), F32)
    for j in range(taps):
        acc = acc + buf[base + j:base + j + ROW_CHUNK, c0:c1] * w_ref[j:j + 1, c0:c1]
    return acc


SUBLANES = 8


def _conformer_body(a_ref, ap_ref, an_ref, g_ref, gp_ref, gn_ref, w_ref, b_ref, lw_ref, lb_ref, o_ref, buf, phase,
                    *, tm, s_tiles, n_tiles):
    i = pl.program_id(1)
    prev_ok, next_ok = _halo_ok(i, s_tiles, n_tiles)

    def glu(a, g):
        return a[...].astype(F32) * jax.nn.sigmoid(g[...].astype(F32))

    buf[0:HALO, :] = jnp.where(prev_ok, glu(ap_ref, gp_ref), 0.0)
    buf[HALO:HALO + tm, :] = glu(a_ref, g_ref)
    buf[HALO + tm:2 * HALO + tm, :] = jnp.where(next_ok, glu(an_ref, gn_ref), 0.0)
    n_rows = tm + 2 * HALO - SUBLANES
    for p in range(SUBLANES):
        phase[p, 0:n_rows, :] = buf[p:p + n_rows, :]
    base = HALO - CONV_K // 2
    for r0 in range(0, tm, ROW_CHUNK):
        acc = jnp.zeros((ROW_CHUNK, CONV_CH), F32)
        for j in range(CONV_K):
            o = base + j
            lo = r0 + o - o % SUBLANES
            acc = acc + phase[o % SUBLANES, lo:lo + ROW_CHUNK, :] * w_ref[j:j + 1, :]
        y = acc + b_ref[...]
        mu = jnp.mean(y, axis=-1, keepdims=True)
        yc = y - mu
        var = jnp.mean(yc * yc, axis=-1, keepdims=True)
        z = yc * lax.rsqrt(var + 1e-5) * lw_ref[...] + lb_ref[...]
        o_ref[r0:r0 + ROW_CHUNK, :] = _silu(z).astype(o_ref.dtype)


def _conformer(p, dw_w, dw_b, ln_w, ln_b, s_len):
    b, t, _ = p.shape
    tm = CONV_TM
    ch = CONV_CH
    a_specs = _halo_specs(tm, t, ch, PB_BA)
    g_specs = _halo_specs(tm, t, ch, PB_BG)
    vec = pl.BlockSpec((1, ch), lambda bi, i: (0, 0))
    body = functools.partial(_conformer_body, tm=tm, s_tiles=s_len // tm, n_tiles=t // tm)
    return pl.pallas_call(
        body,
        grid=(b, t // tm),
        in_specs=[*a_specs, *g_specs, pl.BlockSpec((CONV_K, ch), lambda bi, i: (0, 0)), vec, vec, vec],
        out_specs=pl.BlockSpec((None, tm, ch), lambda bi, i: (bi, i, 0)),
        out_shape=jax.ShapeDtypeStruct((b, t, ch), BF16),
        scratch_shapes=[pltpu.VMEM((tm + 2 * HALO, ch), F32), pltpu.VMEM((SUBLANES, tm + 2 * HALO - SUBLANES, ch), F32)],
        compiler_params=_cparams(("arbitrary", "arbitrary")),
        name="conformer_conv",
    )(p, p, p, p, p, p, dw_w, dw_b.reshape(1, ch), ln_w.reshape(1, ch), ln_b.reshape(1, ch))


def _softplus(x):
    return jnp.maximum(x, 0.0) + jnp.log(1.0 + jnp.exp(-jnp.abs(x)))


def _gdnprep_body(x_ref, xp_ref, xn_ref, small_ref, w_ref, alog_ref, dtb_ref, q_ref, k_ref, v_ref, gb_ref, buf,
                  *, tm, s_tiles, n_tiles):
    i = pl.program_id(1)
    prev_ok, next_ok = _halo_ok(i, s_tiles, n_tiles)
    buf[0:HALO, :] = jnp.where(prev_ok, xp_ref[...].astype(F32), 0.0)
    buf[HALO:HALO + tm, :] = x_ref[...].astype(F32)
    buf[HALO + tm:2 * HALO + tm, :] = jnp.where(next_ok, xn_ref[...].astype(F32), 0.0)
    outs = (q_ref, k_ref, v_ref)
    for r0 in range(0, tm, ROW_CHUNK):
        for part in range(3):
            y = _silu(_conv_rows(buf, w_ref, r0, SHORT_K, part * C_WIDTH, (part + 1) * C_WIDTH))
            if part < 2:
                post = C_HEAD_DIM ** -0.5 if part == 0 else 1.0
                for h in range(C_HEADS):
                    yh = y[:, h * C_HEAD_DIM:(h + 1) * C_HEAD_DIM]
                    n = yh * lax.rsqrt(jnp.sum(yh * yh, axis=-1, keepdims=True) + 1e-6)
                    outs[part][r0:r0 + ROW_CHUNK, h * C_HEAD_DIM:(h + 1) * C_HEAD_DIM] = n * post
            else:
                outs[part][r0:r0 + ROW_CHUNK, :] = y
    sm = small_ref[...]
    lane = lax.broadcasted_iota(jnp.int32, sm.shape, 1)
    gdec = -jnp.exp(alog_ref[...]) * _softplus(sm + dtb_ref[...])
    gb_ref[...] = jnp.where(lane < 2 * C_HEADS, jax.nn.sigmoid(sm), gdec)


def _gdn_prep(p, small, conv_w, a_log, dt_bias, s_len):
    b, t, _ = p.shape
    tm = CONV_TM
    w3 = 3 * C_WIDTH
    x_specs = _halo_specs(tm, t, w3, PB_CQKV)
    pad = SMALL_W - 4 * C_HEADS
    alog = jnp.concatenate([jnp.zeros((2 * C_HEADS,), F32), a_log.reshape(-1), jnp.zeros((pad,), F32)]).reshape(1, SMALL_W)
    dtb = jnp.concatenate([jnp.zeros((2 * C_HEADS,), F32), dt_bias.reshape(-1), jnp.zeros((pad,), F32)]).reshape(1, SMALL_W)
    vec = pl.BlockSpec((1, SMALL_W), lambda bi, i: (0, 0))
    out_blk = pl.BlockSpec((None, tm, C_WIDTH), lambda bi, i: (bi, i, 0))
    body = functools.partial(_gdnprep_body, tm=tm, s_tiles=s_len // tm, n_tiles=t // tm)
    return pl.pallas_call(
        body,
        grid=(b, t // tm),
        in_specs=[*x_specs, pl.BlockSpec((None, tm, SMALL_W), lambda bi, i: (bi, i, 0)),
                  pl.BlockSpec((SHORT_K, w3), lambda bi, i: (0, 0)), vec, vec],
        out_specs=[out_blk, out_blk, out_blk, pl.BlockSpec((None, tm, SMALL_W), lambda bi, i: (bi, i, 0))],
        out_shape=[jax.ShapeDtypeStruct((b, t, C_WIDTH), F32)] * 3 + [jax.ShapeDtypeStruct((b, t, SMALL_W), F32)],
        scratch_shapes=[pltpu.VMEM((tm + 2 * HALO, w3), F32)],
        compiler_params=_cparams(("arbitrary", "arbitrary")),
        name="gdn_prep",
    )(p, p, p, small, conv_w, alog, dtb)


def _mm(a, b):
    return jnp.dot(a.astype(BF16), b.astype(BF16), preferred_element_type=F32)


def _mm_nt(a, b):
    return lax.dot_general(a.astype(BF16), b.astype(BF16), (((1,), (1,)), ((), ())), preferred_element_type=F32)


def _mm_tn(a, b):
    return lax.dot_general(a.astype(BF16), b.astype(BF16), (((0,), (0,)), ((), ())), preferred_element_type=F32)


def _gdn_chunks(chains):
    n = len(chains)
    c = chains[0][0].shape[0]
    ii = lax.broadcasted_iota(jnp.int32, (c, c), 0)
    jj = lax.broadcasted_iota(jnp.int32, (c, c), 1)
    eye = (ii == jj).astype(F32)
    incl = [(ii <= jj) if ch[7] else (ii >= jj) for ch in chains]
    strict = [(ii < jj) if ch[7] else (ii > jj) for ch in chains]
    q = [ch[0] for ch in chains]
    k = [ch[1] for ch in chains]
    v = [ch[2] for ch in chains]
    gcol = [ch[3] for ch in chains]
    bcol = [ch[5] for ch in chains]
    state = [ch[6] for ch in chains]
    decay = [jnp.where(incl[i], jnp.exp(jnp.where(incl[i], gcol[i] - chains[i][4], 0.0)), 0.0) for i in range(n)]
    kb = [k[i] * bcol[i] for i in range(n)]
    low = [jnp.where(strict[i], _mm_nt(kb[i], k[i]) * decay[i], 0.0) for i in range(n)]
    intra = [jnp.where(incl[i], _mm_nt(q[i], k[i]) * decay[i], 0.0) for i in range(n)]
    pw = [-low[i] for i in range(n)]
    t_inv = [eye + pw[i] for i in range(n)]
    for _ in range(int(math.log2(c)) - 1):
        pw = [_mm(pw[i], pw[i]) for i in range(n)]
        t_inv = [t_inv[i] + _mm(t_inv[i], pw[i]) for i in range(n)]
    eg = [jnp.exp(gcol[i]) for i in range(n)]
    u = [_mm(t_inv[i], v[i] * bcol[i]) for i in range(n)]
    w = [_mm(t_inv[i], kb[i] * eg[i]) for i in range(n)]
    g_last = [gcol[i][0:1, :] if chains[i][7] else gcol[i][c - 1:c, :] for i in range(n)]
    k_dec = [k[i] * jnp.exp(g_last[i] - gcol[i]) for i in range(n)]
    q_dec = [q[i] * eg[i] for i in range(n)]
    v_new = [u[i] - _mm(w[i], state[i]) for i in range(n)]
    o = [_mm(q_dec[i], state[i]) + _mm(intra[i], v_new[i]) for i in range(n)]
    new_state = [state[i] * jnp.exp(g_last[i]) + _mm_tn(k_dec[i], v_new[i]) for i in range(n)]
    return list(zip(o, new_state))


def _gdn_body(qf_ref, kf_ref, vf_ref, gf_ref, qb_ref, kb_ref, vb_ref, gbb_ref, of_ref, ob_ref, st_scr):
    @pl.when(pl.program_id(1) == 0)
    def _():
        st_scr[...] = jnp.zeros(st_scr.shape, F32)

    c = GDN_CHUNK
    ii = lax.broadcasted_iota(jnp.int32, (c, c), 0)
    jj = lax.broadcasted_iota(jnp.int32, (c, c), 1)
    dirs = ((qf_ref, kf_ref, vf_ref, gf_ref, of_ref, False), (qb_ref, kb_ref, vb_ref, gbb_ref, ob_ref, True))
    chains = []
    for d, (q_ref, k_ref, v_ref, g_ref, o_ref, rev) in enumerate(dirs):
        gb = g_ref[...]
        tri = ((ii <= jj) if rev else (ii >= jj)).astype(F32)
        gc = jnp.dot(tri, gb, preferred_element_type=F32, precision=HIGHEST)
        gct = gc.T
        for h in range(C_HEADS):
            sl = slice(h * C_HEAD_DIM, (h + 1) * C_HEAD_DIM)
            bc = d * C_HEADS + h
            gcn = 2 * C_HEADS + bc
            chains.append((q_ref[:, sl], k_ref[:, sl], v_ref[:, sl], gc[:, gcn:gcn + 1], gct[gcn:gcn + 1, :],
                           gb[:, bc:bc + 1], st_scr[bc], rev))
    results = _gdn_chunks(chains)
    for d, (_, _, _, _, o_ref, _) in enumerate(dirs):
        for h in range(C_HEADS):
            bc = d * C_HEADS + h
            o, st = results[bc]
            o_ref[:, h * C_HEAD_DIM:(h + 1) * C_HEAD_DIM] = o
            st_scr[bc] = st


def _gdn_scan(q, k, v, gb, s_len):
    b, t, w = q.shape
    c = GDN_CHUNK
    n_lat = s_len // c
    n_all = t // c
    n_ctx = n_all - n_lat

    def fwd(bi, s):
        return (bi, jnp.where(s < n_ctx, n_lat + s, s - n_ctx), 0)

    def bwd(bi, s):
        return (bi, jnp.where(s < n_ctx, n_all - 1 - s, n_all - 1 - s), 0)

    wide_f = pl.BlockSpec((None, c, w), fwd)
    wide_b = pl.BlockSpec((None, c, w), bwd)
    nar_f = pl.BlockSpec((None, c, SMALL_W), fwd)
    nar_b = pl.BlockSpec((None, c, SMALL_W), bwd)
    return pl.pallas_call(
        _gdn_body,
        grid=(b, n_all),
        in_specs=[wide_f, wide_f, wide_f, nar_f, wide_b, wide_b, wide_b, nar_b],
        out_specs=[wide_f, wide_b],
        out_shape=[jax.ShapeDtypeStruct((b, t, w), F32)] * 2,
        scratch_shapes=[pltpu.VMEM((2 * C_HEADS, C_HEAD_DIM, C_HEAD_DIM), F32)],
        compiler_params=_cparams(("arbitrary", "arbitrary")),
        name="gdn_scan",
    )(q, k, v, gb, q, k, v, gb)


def _merge_body(x_ref, yal_ref, yac_ref, yb_ref, of_ref, ob_ref, z_ref, g0_ref, g1_ref, g2_ref, gnw_ref, wb_ref,
                wo_ref, gate_ref, n2_ref, sh_ref, sc_ref, rw_ref, xo_ref, h2_ref, sco_ref, *, tm, s_len):
    i = pl.program_id(1)
    is_ctx = _rows_are_ctx(i, tm, s_len)
    ya = jnp.where(i * tm >= s_len, yac_ref[...], yal_ref[...])
    o = of_ref[...] + ob_ref[...]
    z = z_ref[...].astype(F32)
    parts = []
    for h in range(C_HEADS):
        sl = slice(h * C_HEAD_DIM, (h + 1) * C_HEAD_DIM)
        oh = o[:, sl]
        ms = jnp.mean(oh * oh, axis=-1, keepdims=True)
        parts.append((oh * lax.rsqrt(ms + EPS) * gnw_ref[...] * _silu(z[:, sl])).astype(BF16))
    yc = jnp.concatenate(parts, axis=-1)
    m = jnp.zeros((tm, D_MODEL), F32)
    for bi, (y, g_ref) in enumerate(((ya, g0_ref), (yb_ref[...], g1_ref), (yc, g2_ref))):
        m = m + jax.nn.sigmoid(g_ref[...].astype(F32)) * jnp.dot(y, wb_ref[bi], preferred_element_type=F32)
    upd = jnp.dot(m.astype(BF16), wo_ref[...], preferred_element_type=F32)
    x = x_ref[...] + _pick_mod(gate_ref, is_ctx) * upd
    xo_ref[...] = x
    h2 = _rms_modulate(x, n2_ref[...], _pick_mod(sh_ref, is_ctx), _pick_mod(sc_ref, is_ctx))
    h2_ref[...] = h2
    sco_ref[...] = jax.nn.sigmoid(jnp.dot(h2, rw_ref[...], preferred_element_type=F32, precision=HIGHEST))


def _merge(xs, ya_lat, ya_ctx, yb, o_f, o_b, p, gdn_norm_w, w_branch, w_out, modb, norm2_w, router_w, s_len):
    b, t, d = xs.shape
    tm = CONV_TM
    bw = BRANCH_W
    e_pad = LANES
    s_tiles = s_len // tm
    n_tiles = t // tm
    lat_spec = pl.BlockSpec((None, tm, bw), lambda bi, i: (bi, jnp.minimum(i, s_tiles - 1), 0))
    ctx_spec = pl.BlockSpec((None, tm, bw), lambda bi, i: (bi, jnp.maximum(i - s_tiles, 0), 0))
    rw = jnp.zeros((d, e_pad), F32).at[:, :N_EXPERTS].set(router_w)
    tok = lambda width, blk: pl.BlockSpec((None, tm, width), lambda bi, i: (bi, i, blk))
    modspec = lambda k: pl.BlockSpec((None, 2, d), lambda bi, i: (bi, 0, k))
    full = lambda shape: pl.BlockSpec(shape, lambda bi, i: (0,) * len(shape))
    body = functools.partial(_merge_body, tm=tm, s_len=s_len)
    return pl.pallas_call(
        body,
        grid=(b, t // tm),
        in_specs=[
            tok(d, 0), lat_spec, ctx_spec, tok(bw, 0), tok(bw, 0), tok(bw, 0), tok(bw, PB_CZ),
            tok(d, 0), tok(d, 1), tok(d, 2),
            full((1, C_HEAD_DIM)), full((N_BRANCH, bw, d)), full((d, d)),
            modspec(2), full((1, d)), modspec(3), modspec(4), full((d, e_pad)),
        ],
        out_specs=[tok(d, 0), tok(d, 0), tok(e_pad, 0)],
        out_shape=[jax.ShapeDtypeStruct((b, t, d), F32), jax.ShapeDtypeStruct((b, t, d), F32),
                   jax.ShapeDtypeStruct((b, t, e_pad), F32)],
        compiler_params=_cparams(("arbitrary", "arbitrary")),
        name="merge_norm_router",
    )(xs, ya_lat, ya_ctx, yb, o_f, o_b, p, p, p, p, gdn_norm_w.reshape(1, C_HEAD_DIM), w_branch, w_out,
      modb, norm2_w.reshape(1, d), modb, modb, rw)


MOE_ROWS = 256


def _routing_tables(scores, router_bias):
    n = scores.shape[0]
    rb = MOE_ROWS
    grp = (scores + router_bias.astype(F32)).reshape(n, N_GROUPS, EXPERTS_PER_GROUP)

    def top2(a):
        ids = lax.broadcasted_iota(jnp.int32, a.shape, a.ndim - 1)
        i1 = jnp.argmax(a, axis=-1)
        v1 = jnp.max(a, axis=-1)
        rest = jnp.where(ids == i1[..., None], -jnp.inf, a)
        i2 = jnp.argmax(rest, axis=-1)
        v2 = jnp.max(rest, axis=-1)
        return (v1, v2), jnp.stack([i1, i2], axis=-1).astype(jnp.int32)

    (g1, g2), _ = top2(grp)
    best = jnp.argmax(g1 + g2, axis=-1)
    in_grp = jnp.take_along_axis(grp, best[:, None, None], axis=1)[:, 0]
    _, local = top2(in_grp)
    expert = best[:, None] * EXPERTS_PER_GROUP + local
    w = jnp.take_along_axis(scores, expert, axis=-1)
    gate = w / jnp.sum(w, axis=-1, keepdims=True)
    lo = jnp.min(local, axis=-1)
    hi = jnp.max(local, axis=-1)
    first_is_lo = local[:, 0] < local[:, 1]
    g_lo = jnp.where(first_is_lo, gate[:, 0], gate[:, 1])
    g_hi = jnp.where(first_is_lo, gate[:, 1], gate[:, 0])
    pair = lo * (2 * EXPERTS_PER_GROUP - lo - 1) // 2 + (hi - lo - 1)
    cls = (best * N_PAIR + pair).astype(jnp.int32)

    blk = LANES
    onehot = (cls[:, None] == jnp.arange(N_CLASS, dtype=jnp.int32)[None, :]).astype(F32).reshape(n // blk, blk, N_CLASS)
    tril = (jnp.arange(blk)[:, None] >= jnp.arange(blk)[None, :]).astype(F32)
    within = jnp.einsum("ij,bjc->bic", tril, onehot)
    blk_tot = within[:, -1, :]
    blk_off = jnp.cumsum(blk_tot, axis=0) - blk_tot
    counts = (blk_off[-1] + blk_tot[-1]).astype(jnp.int32)
    rank = jnp.sum(onehot * (within - 1.0 + blk_off[:, None, :]), axis=-1).reshape(n).astype(jnp.int32)
    padded = (counts + rb - 1) // rb * rb
    pad_end = jnp.cumsum(padded)
    pad_start = (pad_end - padded).astype(F32)
    dest = (jnp.sum(onehot * pad_start[None, None, :], axis=-1).reshape(n).astype(jnp.int32) + rank)
    n_blocks = n // rb + N_CLASS
    n_rows = n_blocks * rb
    row_tok = jnp.zeros((n_rows,), jnp.int32).at[dest].set(jnp.arange(n, dtype=jnp.int32))
    tok_row = dest
    row_gate = jnp.zeros((n_rows, LANES), F32).at[dest, 0].set(g_lo).at[dest, 1].set(g_hi)
    n_used = (pad_end[-1] // rb).astype(jnp.int32)
    blk = jnp.minimum(jnp.arange(n_blocks, dtype=jnp.int32), n_used - 1)
    blk_cls = jnp.minimum(jnp.searchsorted(pad_end, blk * rb, side="right"), N_CLASS - 1).astype(jnp.int32)
    pl_lo, pl_hi = [], []
    for a in range(EXPERTS_PER_GROUP):
        for c in range(a + 1, EXPERTS_PER_GROUP):
            pl_lo.append(a)
            pl_hi.append(c)
    pl_lo = jnp.asarray(pl_lo, jnp.int32)
    pl_hi = jnp.asarray(pl_hi, jnp.int32)
    blk_grp = blk_cls // N_PAIR
    blk_pair = blk_cls % N_PAIR
    e_lo = blk_grp * EXPERTS_PER_GROUP + pl_lo[blk_pair]
    e_hi = blk_grp * EXPERTS_PER_GROUP + pl_hi[blk_pair]
    return e_lo, e_hi, n_used.reshape(1), row_tok, tok_row, row_gate


def _row_copy(idx_ref, pos, r, src_hbm, dst, sem):
    return pltpu.make_async_copy(src_hbm.at[pl.ds(idx_ref[pos], 1), :], dst.at[pl.ds(r, 1), :], sem)


def _start_row_gather(idx_ref, base, n_rows, src_hbm, dst, sem):
    for r in range(n_rows):
        _row_copy(idx_ref, base + r, r, src_hbm, dst, sem).start()


def _wait_row_gather(n_rows, src_hbm, dst, sem):
    pltpu.make_async_copy(src_hbm.at[pl.ds(0, n_rows), :], dst, sem).wait()


def _moe_body(elo_ref, ehi_ref, nused_ref, rowtok_ref, h2_hbm, gate_ref, wg_lo, wu_lo, wd_lo, wg_hi, wu_hi, wd_hi,
              y_ref, xbuf, sem, *, rb, n_blocks):
    i = pl.program_id(0)
    n_used = nused_ref[0]
    slot = i % 2

    @pl.when(i == 0)
    def _():
        _start_row_gather(rowtok_ref, 0, rb, h2_hbm, xbuf.at[0], sem.at[0])

    @pl.when(i <= n_used)
    def _():
        _wait_row_gather(rb, h2_hbm, xbuf.at[slot], sem.at[slot])

    @pl.when(i < n_used)
    def _():
        nxt = jnp.minimum(i + 1, n_blocks - 1)
        _start_row_gather(rowtok_ref, nxt * rb, rb, h2_hbm, xbuf.at[1 - slot], sem.at[1 - slot])
        x = xbuf[slot].astype(BF16)
        gate = gate_ref[...]

        def ffn(wg, wu, wd):
            hg = jnp.dot(x, wg[...], preferred_element_type=F32)
            hu = jnp.dot(x, wu[...], preferred_element_type=F32)
            return jnp.dot((_silu(hg) * hu).astype(BF16), wd[...], preferred_element_type=F32)

        y_ref[...] = gate[:, 0:1] * ffn(wg_lo, wu_lo, wd_lo) + gate[:, 1:2] * ffn(wg_hi, wu_hi, wd_hi)

    @pl.when(i >= n_used)
    def _():
        y_ref[...] = jnp.zeros(y_ref.shape, F32)

    @pl.when(jnp.logical_and(i == n_blocks - 1, i < n_used))
    def _():
        _wait_row_gather(rb, h2_hbm, xbuf.at[1 - slot], sem.at[1 - slot])


def _moe_ffn(h2_flat, tables, w_gate, w_up, w_down):
    e_lo, e_hi, n_used, row_tok, _, row_gate = tables
    n, d = h2_flat.shape
    rb = MOE_ROWS
    n_blocks = e_lo.shape[0]
    de = w_gate.shape[-1]
    w_in_lo = pl.BlockSpec((None, d, de), lambda i, lo, hi, nu, rt: (lo[i], 0, 0))
    w_in_hi = pl.BlockSpec((None, d, de), lambda i, lo, hi, nu, rt: (hi[i], 0, 0))
    w_dn_lo = pl.BlockSpec((None, de, d), lambda i, lo, hi, nu, rt: (lo[i], 0, 0))
    w_dn_hi = pl.BlockSpec((None, de, d), lambda i, lo, hi, nu, rt: (hi[i], 0, 0))
    grid_spec = pltpu.PrefetchScalarGridSpec(
        num_scalar_prefetch=4,
        grid=(n_blocks,),
        in_specs=[
            pl.BlockSpec(memory_space=pl.ANY),
            pl.BlockSpec((rb, LANES), lambda i, lo, hi, nu, rt: (i, 0)),
            w_in_lo, w_in_lo, w_dn_lo, w_in_hi, w_in_hi, w_dn_hi,
        ],
        out_specs=pl.BlockSpec((rb, d), lambda i, lo, hi, nu, rt: (i, 0)),
        scratch_shapes=[pltpu.VMEM((2, rb, d), F32), pltpu.SemaphoreType.DMA((2,))],
    )
    return pl.pallas_call(
        functools.partial(_moe_body, rb=rb, n_blocks=n_blocks),
        grid_spec=grid_spec,
        out_shape=jax.ShapeDtypeStruct((n_blocks * rb, d), F32),
        compiler_params=_cparams(("arbitrary",)),
        name="moe_expert_ffn",
    )(e_lo, e_hi, n_used, row_tok, h2_flat, row_gate, w_gate, w_up, w_down, w_gate, w_up, w_down)


def _combine_body(tokrow_ref, x_ref, gate_ref, y_hbm, o_ref, ybuf, sem, *, tm, s_len, tiles_per_batch, n_steps):
    bi = pl.program_id(0)
    i = pl.program_id(1)
    step = bi * tiles_per_batch + i
    slot = step % 2

    @pl.when(step == 0)
    def _():
        _start_row_gather(tokrow_ref, 0, tm, y_hbm, ybuf.at[0], sem.at[0])

    _wait_row_gather(tm, y_hbm, ybuf.at[slot], sem.at[slot])

    @pl.when(step + 1 < n_steps)
    def _():
        _start_row_gather(tokrow_ref, (step + 1) * tm, tm, y_hbm, ybuf.at[1 - slot], sem.at[1 - slot])

    is_ctx = _rows_are_ctx(i, tm, s_len)
    o_ref[...] = x_ref[...] + _pick_mod(gate_ref, is_ctx) * ybuf[slot]


def _combine(xs, y_sorted, tok_row, modb, s_len):
    b, t, d = xs.shape
    tm = CONV_TM
    grid_spec = pltpu.PrefetchScalarGridSpec(
        num_scalar_prefetch=1,
        grid=(b, t // tm),
        in_specs=[
            pl.BlockSpec((None, tm, d), lambda bi, i, tr: (bi, i, 0)),
            pl.BlockSpec((None, 2, d), lambda bi, i, tr: (bi, 0, 5)),
            pl.BlockSpec(memory_space=pl.ANY),
        ],
        out_specs=pl.BlockSpec((None, tm, d), lambda bi, i, tr: (bi, i, 0)),
        scratch_shapes=[pltpu.VMEM((2, tm, d), F32), pltpu.SemaphoreType.DMA((2,))],
    )
    body = functools.partial(_combine_body, tm=tm, s_len=s_len, tiles_per_batch=t // tm, n_steps=b * (t // tm))
    return pl.pallas_call(
        body,
        grid_spec=grid_spec,
        out_shape=jax.ShapeDtypeStruct((b, t, d), F32),
        compiler_params=_cparams(("arbitrary", "arbitrary")),
        name="moe_combine_residual",
    )(tok_row, xs, modb, y_sorted)


def _rope_tables(s_len, ctx_len):
    t = jnp.arange(s_len, dtype=jnp.int32)
    r = (t // GRID_W).astype(F32)
    col = (t % GRID_W).astype(F32)
    axis_dim = A_HEAD_DIM // 2
    inv_freq = ROPE_THETA ** (-jnp.arange(0, axis_dim, 2, dtype=F32) / axis_dim)
    ang_r = r[:, None] * inv_freq
    ang_c = col[:, None] * inv_freq
    cos = jnp.concatenate([jnp.cos(ang_r), jnp.cos(ang_r), jnp.cos(ang_c), jnp.cos(ang_c)], axis=-1)
    sin = jnp.concatenate([-jnp.sin(ang_r), jnp.sin(ang_r), -jnp.sin(ang_c), jnp.sin(ang_c)], axis=-1)
    reps = A_WIDTH // A_HEAD_DIM
    cos = jnp.concatenate([jnp.tile(cos, (1, reps)), jnp.ones((ctx_len, A_WIDTH), F32)], axis=0)
    sin = jnp.concatenate([jnp.tile(sin, (1, reps)), jnp.zeros((ctx_len, A_WIDTH), F32)], axis=0)
    return cos, sin


def _split_w_in(w_in_l):
    oa, oc, ob = A_COLS, A_COLS + C_COLS, A_COLS + C_COLS + B_COLS
    a = w_in_l[:, :oa]
    c_main = w_in_l[:, oa:oa + 4 * C_WIDTH]
    c_small = w_in_l[:, oa + 4 * C_WIDTH:oc]
    bb = w_in_l[:, oc:ob]
    gates = w_in_l[:, ob:]
    main = jnp.concatenate([gates, a, c_main, bb], axis=1).astype(BF16)
    small = jnp.zeros((w_in_l.shape[0], SMALL_W), F32).at[:, :4 * C_HEADS].set(c_small).astype(BF16)
    return main, small


def kernel(x, c, ctx, c_ctx, w_mod, b_mod, norm1_w, norm2_w, w_in, qn_w, kn_w, lam_p, subln_w, conv_dw_w, conv_dw_b,
           conv_ln_w, conv_ln_b, gdn_conv_w, gdn_a_log, gdn_dt_bias, gdn_norm_w, w_branch, w_out, router_w,
           router_bias, w_gate, w_up, w_down):
    b, s_len, d = x.shape
    ctx_len = ctx.shape[1]
    depth = w_mod.shape[0]
    t = s_len + ctx_len
    assert s_len % CONV_TM == 0 and ctx_len % CONV_TM == 0 and s_len % GRID_W == 0

    xs = jnp.concatenate([x, ctx], axis=1)
    rows = -(-(b + 1) // 8) * 8
    cvec = jnp.zeros((rows, d), F32).at[:b].set(c).at[b].set(c_ctx)
    mod = _modulation(cvec, w_mod, b_mod)
    rope_c, rope_s = _rope_tables(s_len, ctx_len)

    for l in range(depth):
        lam_init = 0.8 - 0.6 * math.exp(-0.3 * l)
        modb = jnp.stack([mod[l, :b], jnp.broadcast_to(mod[l, b], (b, N_MOD * d))], axis=1)
        w_main, w_small = _split_w_in(w_in[l])

        p, small = _in_projection(xs, norm1_w[l], modb, w_main, w_small, s_len)
        qt, k, vt = _attn_prep(p, rope_c, rope_s, qn_w[l], kn_w[l])
        ya_lat, ya_ctx = _diff_attention(lam_p[l], qt, k, vt, subln_w[l], s_len, lam_init)
        yb = _conformer(p, conv_dw_w[l], conv_dw_b[l], conv_ln_w[l], conv_ln_b[l], s_len)
        gq, gk, gv, gb = _gdn_prep(p, small, gdn_conv_w[l], gdn_a_log[l], gdn_dt_bias[l], s_len)
        o_f, o_b = _gdn_scan(gq, gk, gv, gb, s_len)
        xs, h2, scores = _merge(xs, ya_lat, ya_ctx, yb, o_f, o_b, p, gdn_norm_w[l], w_branch[l].astype(BF16),
                                w_out[l].astype(BF16), modb, norm2_w[l], router_w, s_len)

        tables = _routing_tables(scores.reshape(b * t, -1)[:, :N_EXPERTS], router_bias)
        y_sorted = _moe_ffn(h2.reshape(b * t, d), tables, w_gate[l].astype(BF16), w_up[l].astype(BF16),
                            w_down[l].astype(BF16))
        xs = _combine(xs, y_sorted, tables[4], modb, s_len)
    return xs[:, :s_len]
```

```python
import functools
import math

import jax
import jax.numpy as jnp
from jax import lax
from jax.experimental import pallas as pl
from jax.experimental.pallas import tpu as pltpu

F32 = jnp.float32
BF16 = jnp.bfloat16
HIGHEST = lax.Precision.HIGHEST

D_MODEL = 1024
GRID_W = 64
N_MOD = 6
EPS = 1e-6

A_HEADS = 4
A_HEAD_DIM = 64
A_V_DIM = 2 * A_HEAD_DIM
A_WIDTH = A_HEADS * A_V_DIM
ROPE_THETA = 10000.0
SUBLN_EPS = 1e-5

CONV_CH = 512
CONV_K = 31

C_HEADS = 4
C_HEAD_DIM = 128
C_WIDTH = C_HEADS * C_HEAD_DIM
SHORT_K = 5
GDN_CHUNK = 128
NEUMANN_BLK = 64

N_BRANCH = 3
BRANCH_W = 512

A_COLS = 3 * A_WIDTH
C_COLS = 4 * C_WIDTH + 4 * C_HEADS
B_COLS = 2 * CONV_CH
GATE_COLS = N_BRANCH * D_MODEL

N_EXPERTS = 16
N_GROUPS = 4
EXPERTS_PER_GROUP = N_EXPERTS // N_GROUPS
TOP_K = 2
D_EXPERT = 1024
N_PAIR = EXPERTS_PER_GROUP * (EXPERTS_PER_GROUP - 1) // 2
N_CLASS = N_GROUPS * N_PAIR

LANES = 128
HALO = 16
CONV_TM = 256
ROW_CHUNK = 32
VMEM_LIMIT = 52 * 1024 * 1024

P_COLS = GATE_COLS + A_COLS + 4 * C_WIDTH + B_COLS
PB_AQ, PB_AK, PB_AV = 6, 7, 8
PB_CQKV = 3
PB_CZ, PB_BA, PB_BG = 12, 13, 14
SMALL_W = LANES


def _cparams(sem):
    return pltpu.CompilerParams(dimension_semantics=sem, vmem_limit_bytes=VMEM_LIMIT)


def _silu(x):
    return x * jax.nn.sigmoid(x)


def _pick(n, prefs):
    for p in prefs:
        if n % p == 0:
            return p
    raise ValueError(f"no tile in {prefs} divides {n}")


def _mod_body(cv_ref, w_ref, b_ref, o_ref):
    s = _silu(cv_ref[...])
    o_ref[...] = jnp.dot(s, w_ref[...], preferred_element_type=F32, precision=HIGHEST) + b_ref[...]


def _modulation(cvec, w_mod, b_mod):
    depth, d, n = w_mod.shape
    r = cvec.shape[0]
    tn = _pick(n, (1536, 1024, 512, 128))
    return pl.pallas_call(
        _mod_body,
        grid=(depth, n // tn),
        in_specs=[
            pl.BlockSpec((r, d), lambda l, j: (0, 0)),
            pl.BlockSpec((None, d, tn), lambda l, j: (l, 0, j)),
            pl.BlockSpec((None, 1, tn), lambda l, j: (l, 0, j)),
        ],
        out_specs=pl.BlockSpec((None, r, tn), lambda l, j: (l, 0, j)),
        out_shape=jax.ShapeDtypeStruct((depth, r, n), F32),
        compiler_params=_cparams(("arbitrary", "arbitrary")),
        name="modulation",
    )(cvec, w_mod, b_mod.reshape(depth, 1, n))


def _rows_are_ctx(tile_idx, tm, s_len):
    rows = tile_idx * tm + lax.broadcasted_iota(jnp.int32, (tm, 1), 0)
    return rows >= s_len


def _pick_mod(mod_ref, is_ctx):
    return jnp.where(is_ctx, mod_ref[1:2, :], mod_ref[0:1, :])


def _rms_modulate(x, w, shift, scale):
    ms = jnp.mean(x * x, axis=-1, keepdims=True)
    return x * lax.rsqrt(ms + EPS) * w * (1.0 + scale) + shift


def _inproj_body(x_ref, nw_ref, sh_ref, sc_ref, w_ref, ws_ref, p_ref, small_ref, h_scr, *, tm, s_len):
    i = pl.program_id(1)
    j = pl.program_id(2)

    @pl.when(j == 0)
    def _():
        is_ctx = _rows_are_ctx(i, tm, s_len)
        h = _rms_modulate(x_ref[...], nw_ref[...], _pick_mod(sh_ref, is_ctx), _pick_mod(sc_ref, is_ctx))
        h = h.astype(BF16)
        h_scr[...] = h
        small_ref[...] = jnp.dot(h, ws_ref[...], preferred_element_type=F32)

    p_ref[...] = jnp.dot(h_scr[...], w_ref[...], preferred_element_type=F32).astype(BF16)


def _in_projection(xs, norm_w, modb, w_main, w_small, s_len):
    b, t, d = xs.shape
    n = w_main.shape[1]
    tm = _pick(t, (768, 512, 256))
    tn = _pick(n, (3840, 1536, 512))
    body = functools.partial(_inproj_body, tm=tm, s_len=s_len)
    return pl.pallas_call(
        body,
        grid=(b, t // tm, n // tn),
        in_specs=[
            pl.BlockSpec((None, tm, d), lambda bi, i, j: (bi, i, 0)),
            pl.BlockSpec((1, d), lambda bi, i, j: (0, 0)),
            pl.BlockSpec((None, 2, d), lambda bi, i, j: (bi, 0, 0)),
            pl.BlockSpec((None, 2, d), lambda bi, i, j: (bi, 0, 1)),
            pl.BlockSpec((d, tn), lambda bi, i, j: (0, j)),
            pl.BlockSpec((d, SMALL_W), lambda bi, i, j: (0, 0)),
        ],
        out_specs=[
            pl.BlockSpec((None, tm, tn), lambda bi, i, j: (bi, i, j)),
            pl.BlockSpec((None, tm, SMALL_W), lambda bi, i, j: (bi, i, 0)),
        ],
        out_shape=[
            jax.ShapeDtypeStruct((b, t, n), BF16),
            jax.ShapeDtypeStruct((b, t, SMALL_W), F32),
        ],
        scratch_shapes=[pltpu.VMEM((tm, d), BF16)],
        compiler_params=_cparams(("arbitrary", "arbitrary", "arbitrary")),
        name="in_projection",
    )(xs, norm_w.reshape(1, d), modb, modb, w_main, w_small)


def _group_mean(x2, bd):
    hi = x2.astype(BF16)
    lo = (x2 - hi.astype(F32)).astype(BF16)
    return (jnp.dot(hi, bd, preferred_element_type=F32) + jnp.dot(lo, bd, preferred_element_type=F32))


def _rope_partner(y):
    n = y.shape[-1]
    lane = lax.broadcasted_iota(jnp.int32, y.shape, 1)
    return jnp.where((lane & 16) == 0, pltpu.roll(y, n - 16, 1), pltpu.roll(y, 16, 1))


def _attnprep_body(q_ref, k_ref, v_ref, rc_ref, rs_ref, qw_ref, kw_ref, bd_ref, qt_ref, ko_ref, vt_ref):
    bd = bd_ref[...]
    rc = rc_ref[...]
    rs = rs_ref[...]

    def prep(x_ref, w_ref, post):
        x = x_ref[...].astype(F32)
        y = x * lax.rsqrt(_group_mean(x * x, bd) + EPS) * w_ref[...]
        y = y * rc + _rope_partner(y) * rs
        return y * post

    qt_ref[...] = prep(q_ref, qw_ref, A_HEAD_DIM ** -0.5 * math.log2(math.e)).T.astype(BF16)
    ko_ref[...] = prep(k_ref, kw_ref, 1.0).astype(BF16)
    vt_ref[...] = v_ref[...].astype(F32).T.astype(BF16)


def _attn_prep(p, rope_c, rope_s, qn_w, kn_w):
    b, t, _ = p.shape
    w = A_WIDTH
    tm = _pick(t, (768, 512, 256))
    g = jnp.arange(w) // A_HEAD_DIM
    bd = jnp.where(g[:, None] == g[None, :], 1.0 / A_HEAD_DIM, 0.0).astype(BF16)
    reps = w // A_HEAD_DIM
    tok = lambda blk: pl.BlockSpec((None, tm, w), lambda bi, i: (bi, i, blk))
    tr = pl.BlockSpec((None, w, tm), lambda bi, i: (bi, 0, i))
    return pl.pallas_call(
        _attnprep_body,
        grid=(b, t // tm),
        in_specs=[
            tok(PB_AQ), tok(PB_AK), tok(PB_AV),
            pl.BlockSpec((tm, w), lambda bi, i: (i, 0)),
            pl.BlockSpec((tm, w), lambda bi, i: (i, 0)),
            pl.BlockSpec((1, w), lambda bi, i: (0, 0)),
            pl.BlockSpec((1, w), lambda bi, i: (0, 0)),
            pl.BlockSpec((w, w), lambda bi, i: (0, 0)),
        ],
        out_specs=[tr, tok(0), tr],
        out_shape=[jax.ShapeDtypeStruct((b, w, t), BF16), jax.ShapeDtypeStruct((b, t, w), BF16),
                   jax.ShapeDtypeStruct((b, w, t), BF16)],
        compiler_params=_cparams(("arbitrary", "arbitrary")),
        name="attn_prep",
    )(p, p, p, rope_c, rope_s, jnp.tile(qn_w, reps).reshape(1, w), jnp.tile(kn_w, reps).reshape(1, w), bd)


def _attn_body(lam_ref, qt_ref, k_ref, vt_ref, sw_ref, o_ref, m_scr, l_scr, acc_scr, sa_scr, sb_scr,
               *, tq, tk, kv0, nkv, lam_init):
    qt = qt_ref[...]
    row = lax.broadcasted_iota(jnp.int32, qt.shape, 0)
    zero = jnp.zeros_like(qt)
    qm = (jnp.where(row < A_HEAD_DIM, qt, zero), jnp.where(row >= A_HEAD_DIM, qt, zero))
    m_scr[...] = jnp.full(m_scr.shape, -jnp.inf, F32)
    l_scr[...] = jnp.zeros(l_scr.shape, F32)
    acc_scr[...] = jnp.zeros(acc_scr.shape, F32)

    def scores(c, dst):
        off = pl.multiple_of(kv0 + c * tk, tk)
        kc = k_ref[pl.ds(off, tk), :]
        for mi in range(2):
            dst[mi] = jnp.dot(kc, qm[mi], preferred_element_type=F32)

    def accumulate(c, src):
        off = pl.multiple_of(kv0 + c * tk, tk)
        vtc = vt_ref[:, pl.ds(off, tk)]
        s = [src[mi] for mi in range(2)]
        m_old = [m_scr[mi] for mi in range(2)]
        m_new = [jnp.maximum(m_old[mi], jnp.max(s[mi], axis=0, keepdims=True)) for mi in range(2)]
        p = [jnp.exp2(s[mi] - m_new[mi]) for mi in range(2)]
        alpha = [jnp.exp2(m_old[mi] - m_new[mi]) for mi in range(2)]
        for mi in range(2):
            l_scr[mi] = alpha[mi] * l_scr[mi] + jnp.sum(p[mi], axis=0, keepdims=True)
            m_scr[mi] = m_new[mi]
        pv = [jnp.dot(vtc, p[mi].astype(BF16), preferred_element_type=F32) for mi in range(2)]
        for mi in range(2):
            acc_scr[mi] = alpha[mi] * acc_scr[mi] + pv[mi]

    scores(0, sa_scr)
    n_pairs = (nkv - 1) // 2

    def pair(i, carry):
        c = 2 * i
        scores(c + 1, sb_scr)
        accumulate(c, sa_scr)
        scores(c + 2, sa_scr)
        accumulate(c + 1, sb_scr)
        return carry

    lax.fori_loop(0, n_pairs, pair, 0)
    if nkv % 2 == 0:
        scores(nkv - 1, sb_scr)
        accumulate(nkv - 2, sa_scr)
        accumulate(nkv - 1, sb_scr)
    else:
        accumulate(nkv - 1, sa_scr)

    lp = lam_ref[...]
    lam = (jnp.exp(jnp.sum(lp[0:1] * lp[1:2], axis=-1, keepdims=True))
           - jnp.exp(jnp.sum(lp[2:3] * lp[3:4], axis=-1, keepdims=True)) + lam_init)
    o = (acc_scr[0] / l_scr[0] - lam * (acc_scr[1] / l_scr[1])).T
    ms = jnp.mean(o * o, axis=-1, keepdims=True)
    o_ref[...] = (o * lax.rsqrt(ms + SUBLN_EPS) * sw_ref[...] * (1.0 - lam_init)).astype(o_ref.dtype)


def _attention_call(lam_p, qt, k, vt, subln_w, *, q0, nq_rows, tq, kv0, nkv_rows, tk, lam_init, name):
    b, t, _ = k.shape
    hd = A_V_DIM
    assert q0 % tq == 0 and nq_rows % tq == 0 and kv0 % tk == 0 and nkv_rows % tk == 0
    qb0 = q0 // tq
    body = functools.partial(_attn_body, tq=tq, tk=tk, kv0=kv0, nkv=nkv_rows // tk, lam_init=lam_init)
    return pl.pallas_call(
        body,
        grid=(b, A_HEADS, nq_rows // tq),
        in_specs=[
            pl.BlockSpec((4, A_HEAD_DIM), lambda bi, h, i: (0, 0)),
            pl.BlockSpec((None, hd, tq), lambda bi, h, i: (bi, h, qb0 + i)),
            pl.BlockSpec((None, t, hd), lambda bi, h, i: (bi, 0, h)),
            pl.BlockSpec((None, hd, t), lambda bi, h, i: (bi, h, 0)),
            pl.BlockSpec((1, hd), lambda bi, h, i: (0, 0)),
        ],
        out_specs=pl.BlockSpec((None, tq, hd), lambda bi, h, i: (bi, i, h)),
        out_shape=jax.ShapeDtypeStruct((b, nq_rows, A_WIDTH), BF16),
        scratch_shapes=[pltpu.VMEM((2, 1, tq), F32), pltpu.VMEM((2, 1, tq), F32), pltpu.VMEM((2, hd, tq), F32),
                        pltpu.VMEM((2, tk, tq), F32), pltpu.VMEM((2, tk, tq), F32)],
        compiler_params=_cparams(("arbitrary", "arbitrary", "arbitrary")),
        name=name,
    )(lam_p, qt, k, vt, subln_w.reshape(1, hd))


def _diff_attention(lam_p, qt, k, vt, subln_w, s_len, lam_init):
    b, t, _ = k.shape
    ctx_len = t - s_len
    tq = _pick(s_len, (512, 256))
    tk = _pick(t, (1408, 768, 256))
    ya_lat = _attention_call(lam_p, qt, k, vt, subln_w, q0=0, nq_rows=s_len, tq=tq,
                             kv0=0, nkv_rows=t, tk=tk, lam_init=lam_init, name="diff_attention_lat")
    tc = _pick(ctx_len, (256, 128))
    assert s_len % tc == 0
    ya_ctx = _attention_call(lam_p, qt, k, vt, subln_w, q0=s_len, nq_rows=ctx_len, tq=tc,
                             kv0=s_len, nkv_rows=ctx_len, tk=tc, lam_init=lam_init, name="diff_attention_ctx")
    return ya_lat, ya_ctx


def _halo_specs(tm, t, width, col_blk):
    per = tm // HALO
    last = t // HALO - 1
    main = pl.BlockSpec((None, tm, width), lambda bi, i: (bi, i, col_blk))
    prev = pl.BlockSpec((None, HALO, width), lambda bi, i: (bi, jnp.maximum(i * per - 1, 0), col_blk))
    nxt = pl.BlockSpec((None, HALO, width), lambda bi, i: (bi, jnp.minimum((i + 1) * per, last), col_blk))
    return main, prev, nxt


def _halo_ok(i, s_tiles, n_tiles):
    prev_ok = jnp.logical_and(i != 0, i != s_tiles)
    next_ok = jnp.logical_and(i != s_tiles - 1, i != n_tiles - 1)
    return prev_ok, next_ok


def _conv_rows(buf, w_ref, r0, taps, c0, c1):
    base = HALO - taps // 2 + r0
    acc = jnp.zeros((ROW_CHUNK, c1 - c0), F32)
    for j in range(taps):
        acc = acc + buf[base + j:base + j + ROW_CHUNK, c0:c1] * w_ref[j:j + 1, c0:c1]
    return acc


SUBLANES = 8


def _conformer_body(a_ref, ap_ref, an_ref, g_ref, gp_ref, gn_ref, w_ref, b_ref, lw_ref, lb_ref, o_ref, buf, phase,
                    *, tm, s_tiles, n_tiles):
    i = pl.program_id(1)
    prev_ok, next_ok = _halo_ok(i, s_tiles, n_tiles)

    def glu(a, g):
        return a[...].astype(F32) * jax.nn.sigmoid(g[...].astype(F32))

    buf[0:HALO, :] = jnp.where(prev_ok, glu(ap_ref, gp_ref), 0.0)
    buf[HALO:HALO + tm, :] = glu(a_ref, g_ref)
    buf[HALO + tm:2 * HALO + tm, :] = jnp.where(next_ok, glu(an_ref, gn_ref), 0.0)
    n_rows = tm + 2 * HALO - SUBLANES
    for p in range(SUBLANES):
        phase[p, 0:n_rows, :] = buf[p:p + n_rows, :]
    base = HALO - CONV_K // 2
    for r0 in range(0, tm, ROW_CHUNK):
        acc = jnp.zeros((ROW_CHUNK, CONV_CH), F32)
        for j in range(CONV_K):
            o = base + j
            lo = r0 + o - o % SUBLANES
            acc = acc + phase[o % SUBLANES, lo:lo + ROW_CHUNK, :] * w_ref[j:j + 1, :]
        y = acc + b_ref[...]
        mu = jnp.mean(y, axis=-1, keepdims=True)
        yc = y - mu
        var = jnp.mean(yc * yc, axis=-1, keepdims=True)
        z = yc * lax.rsqrt(var + 1e-5) * lw_ref[...] + lb_ref[...]
        o_ref[r0:r0 + ROW_CHUNK, :] = _silu(z).astype(o_ref.dtype)


def _conformer(p, dw_w, dw_b, ln_w, ln_b, s_len):
    b, t, _ = p.shape
    tm = CONV_TM
    ch = CONV_CH
    a_specs = _halo_specs(tm, t, ch, PB_BA)
    g_specs = _halo_specs(tm, t, ch, PB_BG)
    vec = pl.BlockSpec((1, ch), lambda bi, i: (0, 0))
    body = functools.partial(_conformer_body, tm=tm, s_tiles=s_len // tm, n_tiles=t // tm)
    return pl.pallas_call(
        body,
        grid=(b, t // tm),
        in_specs=[*a_specs, *g_specs, pl.BlockSpec((CONV_K, ch), lambda bi, i: (0, 0)), vec, vec, vec],
        out_specs=pl.BlockSpec((None, tm, ch), lambda bi, i: (bi, i, 0)),
        out_shape=jax.ShapeDtypeStruct((b, t, ch), BF16),
        scratch_shapes=[pltpu.VMEM((tm + 2 * HALO, ch), F32), pltpu.VMEM((SUBLANES, tm + 2 * HALO - SUBLANES, ch), F32)],
        compiler_params=_cparams(("arbitrary", "arbitrary")),
        name="conformer_conv",
    )(p, p, p, p, p, p, dw_w, dw_b.reshape(1, ch), ln_w.reshape(1, ch), ln_b.reshape(1, ch))


def _softplus(x):
    return jnp.maximum(x, 0.0) + jnp.log(1.0 + jnp.exp(-jnp.abs(x)))


def _gdnprep_body(x_ref, xp_ref, xn_ref, small_ref, w_ref, alog_ref, dtb_ref, q_ref, k_ref, v_ref, gb_ref, buf,
                  *, tm, s_tiles, n_tiles):
    i = pl.program_id(1)
    prev_ok, next_ok = _halo_ok(i, s_tiles, n_tiles)
    buf[0:HALO, :] = jnp.where(prev_ok, xp_ref[...].astype(F32), 0.0)
    buf[HALO:HALO + tm, :] = x_ref[...].astype(F32)
    buf[HALO + tm:2 * HALO + tm, :] = jnp.where(next_ok, xn_ref[...].astype(F32), 0.0)
    outs = (q_ref, k_ref, v_ref)
    for r0 in range(0, tm, ROW_CHUNK):
        for part in range(3):
            y = _silu(_conv_rows(buf, w_ref, r0, SHORT_K, part * C_WIDTH, (part + 1) * C_WIDTH))
            if part < 2:
                post = C_HEAD_DIM ** -0.5 if part == 0 else 1.0
                for h in range(C_HEADS):
                    yh = y[:, h * C_HEAD_DIM:(h + 1) * C_HEAD_DIM]
                    n = yh * lax.rsqrt(jnp.sum(yh * yh, axis=-1, keepdims=True) + 1e-6)
                    outs[part][r0:r0 + ROW_CHUNK, h * C_HEAD_DIM:(h + 1) * C_HEAD_DIM] = n * post
            else:
                outs[part][r0:r0 + ROW_CHUNK, :] = y
    sm = small_ref[...]
    lane = lax.broadcasted_iota(jnp.int32, sm.shape, 1)
    gdec = -jnp.exp(alog_ref[...]) * _softplus(sm + dtb_ref[...])
    gb_ref[...] = jnp.where(lane < 2 * C_HEADS, jax.nn.sigmoid(sm), gdec)


def _gdn_prep(p, small, conv_w, a_log, dt_bias, s_len):
    b, t, _ = p.shape
    tm = CONV_TM
    w3 = 3 * C_WIDTH
    x_specs = _halo_specs(tm, t, w3, PB_CQKV)
    pad = SMALL_W - 4 * C_HEADS
    alog = jnp.concatenate([jnp.zeros((2 * C_HEADS,), F32), a_log.reshape(-1), jnp.zeros((pad,), F32)]).reshape(1, SMALL_W)
    dtb = jnp.concatenate([jnp.zeros((2 * C_HEADS,), F32), dt_bias.reshape(-1), jnp.zeros((pad,), F32)]).reshape(1, SMALL_W)
    vec = pl.BlockSpec((1, SMALL_W), lambda bi, i: (0, 0))
    out_blk = pl.BlockSpec((None, tm, C_WIDTH), lambda bi, i: (bi, i, 0))
    body = functools.partial(_gdnprep_body, tm=tm, s_tiles=s_len // tm, n_tiles=t // tm)
    return pl.pallas_call(
        body,
        grid=(b, t // tm),
        in_specs=[*x_specs, pl.BlockSpec((None, tm, SMALL_W), lambda bi, i: (bi, i, 0)),
                  pl.BlockSpec((SHORT_K, w3), lambda bi, i: (0, 0)), vec, vec],
        out_specs=[out_blk, out_blk, out_blk, pl.BlockSpec((None, tm, SMALL_W), lambda bi, i: (bi, i, 0))],
        out_shape=[jax.ShapeDtypeStruct((b, t, C_WIDTH), F32)] * 3 + [jax.ShapeDtypeStruct((b, t, SMALL_W), F32)],
        scratch_shapes=[pltpu.VMEM((tm + 2 * HALO, w3), F32)],
        compiler_params=_cparams(("arbitrary", "arbitrary")),
        name="gdn_prep",
    )(p, p, p, small, conv_w, alog, dtb)


def _mm(a, b):
    return jnp.dot(a.astype(BF16), b.astype(BF16), preferred_element_type=F32)


def _mm_nt(a, b):
    return lax.dot_general(a.astype(BF16), b.astype(BF16), (((1,), (1,)), ((), ())), preferred_element_type=F32)


def _mm_tn(a, b):
    return lax.dot_general(a.astype(BF16), b.astype(BF16), (((0,), (0,)), ((), ())), preferred_element_type=F32)


def _gdn_chunks(chains):
    n = len(chains)
    c = chains[0][0].shape[0]
    ii = lax.broadcasted_iota(jnp.int32, (c, c), 0)
    jj = lax.broadcasted_iota(jnp.int32, (c, c), 1)
    eye = (ii == jj).astype(F32)
    incl = [(ii <= jj) if ch[7] else (ii >= jj) for ch in chains]
    strict = [(ii < jj) if ch[7] else (ii > jj) for ch in chains]
    q = [ch[0] for ch in chains]
    k = [ch[1] for ch in chains]
    v = [ch[2] for ch in chains]
    gcol = [ch[3] for ch in chains]
    bcol = [ch[5] for ch in chains]
    state = [ch[6] for ch in chains]
    decay = [jnp.where(incl[i], jnp.exp(jnp.where(incl[i], gcol[i] - chains[i][4], 0.0)), 0.0) for i in range(n)]
    kb = [k[i] * bcol[i] for i in range(n)]
    low = [jnp.where(strict[i], _mm_nt(kb[i], k[i]) * decay[i], 0.0) for i in range(n)]
    intra = [jnp.where(incl[i], _mm_nt(q[i], k[i]) * decay[i], 0.0) for i in range(n)]
    blk = min(c, NEUMANN_BLK)
    assert c in (blk, 2 * blk)
    if c > blk:
        same = (ii // blk) == (jj // blk)
        low_d = [jnp.where(same, low[i], 0.0) for i in range(n)]
        low_o = [jnp.where(same, 0.0, low[i]) for i in range(n)]
    else:
        low_d = low
    pw = [-low_d[i] for i in range(n)]
    t_inv = [eye + pw[i] for i in range(n)]
    for _ in range(int(math.log2(blk)) - 1):
        pw = [_mm(pw[i], pw[i]) for i in range(n)]
        t_inv = [t_inv[i] + _mm(t_inv[i], pw[i]) for i in range(n)]
    if c > blk:
        od = [_mm(low_o[i], t_inv[i]) for i in range(n)]
        t_inv = [t_inv[i] - _mm(t_inv[i], od[i]) for i in range(n)]
    eg = [jnp.exp(gcol[i]) for i in range(n)]
    u = [_mm(t_inv[i], v[i] * bcol[i]) for i in range(n)]
    w = [_mm(t_inv[i], kb[i] * eg[i]) for i in range(n)]
    g_last = [gcol[i][0:1, :] if chains[i][7] else gcol[i][c - 1:c, :] for i in range(n)]
    k_dec = [k[i] * jnp.exp(g_last[i] - gcol[i]) for i in range(n)]
    q_dec = [q[i] * eg[i] for i in range(n)]
    v_new = [u[i] - _mm(w[i], state[i]) for i in range(n)]
    o = [_mm(q_dec[i], state[i]) + _mm(intra[i], v_new[i]) for i in range(n)]
    new_state = [state[i] * jnp.exp(g_last[i]) + _mm_tn(k_dec[i], v_new[i]) for i in range(n)]
    return list(zip(o, new_state))


def _gdn_body(qf_ref, kf_ref, vf_ref, gf_ref, qb_ref, kb_ref, vb_ref, gbb_ref, of_ref, ob_ref, st_scr):
    @pl.when(pl.program_id(1) == 0)
    def _():
        st_scr[...] = jnp.zeros(st_scr.shape, F32)

    c = GDN_CHUNK
    ii = lax.broadcasted_iota(jnp.int32, (c, c), 0)
    jj = lax.broadcasted_iota(jnp.int32, (c, c), 1)
    dirs = ((qf_ref, kf_ref, vf_ref, gf_ref, of_ref, False), (qb_ref, kb_ref, vb_ref, gbb_ref, ob_ref, True))
    chains = []
    for d, (q_ref, k_ref, v_ref, g_ref, o_ref, rev) in enumerate(dirs):
        gb = g_ref[...]
        tri = ((ii <= jj) if rev else (ii >= jj)).astype(F32)
        gc = jnp.dot(tri, gb, preferred_element_type=F32, precision=HIGHEST)
        gct = gc.T
        for h in range(C_HEADS):
            sl = slice(h * C_HEAD_DIM, (h + 1) * C_HEAD_DIM)
            bc = d * C_HEADS + h
            gcn = 2 * C_HEADS + bc
            chains.append((q_ref[:, sl], k_ref[:, sl], v_ref[:, sl], gc[:, gcn:gcn + 1], gct[gcn:gcn + 1, :],
                           gb[:, bc:bc + 1], st_scr[bc], rev))
    results = _gdn_chunks(chains)
    for d, (_, _, _, _, o_ref, _) in enumerate(dirs):
        for h in range(C_HEADS):
            bc = d * C_HEADS + h
            o, st = results[bc]
            o_ref[:, h * C_HEAD_DIM:(h + 1) * C_HEAD_DIM] = o
            st_scr[bc] = st


def _gdn_scan(q, k, v, gb, s_len):
    b, t, w = q.shape
    c = GDN_CHUNK
    n_lat = s_len // c
    n_all = t // c
    n_ctx = n_all - n_lat

    def fwd(bi, s):
        return (bi, jnp.where(s < n_ctx, n_lat + s, s - n_ctx), 0)

    def bwd(bi, s):
        return (bi, jnp.where(s < n_ctx, n_all - 1 - s, n_all - 1 - s), 0)

    wide_f = pl.BlockSpec((None, c, w), fwd)
    wide_b = pl.BlockSpec((None, c, w), bwd)
    nar_f = pl.BlockSpec((None, c, SMALL_W), fwd)
    nar_b = pl.BlockSpec((None, c, SMALL_W), bwd)
    return pl.pallas_call(
        _gdn_body,
        grid=(b, n_all),
        in_specs=[wide_f, wide_f, wide_f, nar_f, wide_b, wide_b, wide_b, nar_b],
        out_specs=[wide_f, wide_b],
        out_shape=[jax.ShapeDtypeStruct((b, t, w), F32)] * 2,
        scratch_shapes=[pltpu.VMEM((2 * C_HEADS, C_HEAD_DIM, C_HEAD_DIM), F32)],
        compiler_params=_cparams(("arbitrary", "arbitrary")),
        name="gdn_scan",
    )(q, k, v, gb, q, k, v, gb)


def _merge_body(x_ref, yal_ref, yac_ref, yb_ref, of_ref, ob_ref, z_ref, g0_ref, g1_ref, g2_ref, gnw_ref, wb_ref,
                wo_ref, gate_ref, n2_ref, sh_ref, sc_ref, rw_ref, xo_ref, h2_ref, sco_ref, *, tm, s_len):
    i = pl.program_id(1)
    is_ctx = _rows_are_ctx(i, tm, s_len)
    ya = jnp.where(i * tm >= s_len, yac_ref[...], yal_ref[...])
    o = of_ref[...] + ob_ref[...]
    z = z_ref[...].astype(F32)
    parts = []
    for h in range(C_HEADS):
        sl = slice(h * C_HEAD_DIM, (h + 1) * C_HEAD_DIM)
        oh = o[:, sl]
        ms = jnp.mean(oh * oh, axis=-1, keepdims=True)
        parts.append((oh * lax.rsqrt(ms + EPS) * gnw_ref[...] * _silu(z[:, sl])).astype(BF16))
    yc = jnp.concatenate(parts, axis=-1)
    m = jnp.zeros((tm, D_MODEL), F32)
    for bi, (y, g_ref) in enumerate(((ya, g0_ref), (yb_ref[...], g1_ref), (yc, g2_ref))):
        m = m + jax.nn.sigmoid(g_ref[...].astype(F32)) * jnp.dot(y, wb_ref[bi], preferred_element_type=F32)
    upd = jnp.dot(m.astype(BF16), wo_ref[...], preferred_element_type=F32)
    x = x_ref[...] + _pick_mod(gate_ref, is_ctx) * upd
    xo_ref[...] = x
    h2 = _rms_modulate(x, n2_ref[...], _pick_mod(sh_ref, is_ctx), _pick_mod(sc_ref, is_ctx))
    h2_ref[...] = h2
    sco_ref[...] = jax.nn.sigmoid(jnp.dot(h2, rw_ref[...], preferred_element_type=F32, precision=HIGHEST))


def _merge(xs, ya_lat, ya_ctx, yb, o_f, o_b, p, gdn_norm_w, w_branch, w_out, modb, norm2_w, router_w, s_len):
    b, t, d = xs.shape
    tm = CONV_TM
    bw = BRANCH_W
    e_pad = LANES
    s_tiles = s_len // tm
    n_tiles = t // tm
    lat_spec = pl.BlockSpec((None, tm, bw), lambda bi, i: (bi, jnp.minimum(i, s_tiles - 1), 0))
    ctx_spec = pl.BlockSpec((None, tm, bw), lambda bi, i: (bi, jnp.maximum(i - s_tiles, 0), 0))
    rw = jnp.zeros((d, e_pad), F32).at[:, :N_EXPERTS].set(router_w)
    tok = lambda width, blk: pl.BlockSpec((None, tm, width), lambda bi, i: (bi, i, blk))
    modspec = lambda k: pl.BlockSpec((None, 2, d), lambda bi, i: (bi, 0, k))
    full = lambda shape: pl.BlockSpec(shape, lambda bi, i: (0,) * len(shape))
    body = functools.partial(_merge_body, tm=tm, s_len=s_len)
    return pl.pallas_call(
        body,
        grid=(b, t // tm),
        in_specs=[
            tok(d, 0), lat_spec, ctx_spec, tok(bw, 0), tok(bw, 0), tok(bw, 0), tok(bw, PB_CZ),
            tok(d, 0), tok(d, 1), tok(d, 2),
            full((1, C_HEAD_DIM)), full((N_BRANCH, bw, d)), full((d, d)),
            modspec(2), full((1, d)), modspec(3), modspec(4), full((d, e_pad)),
        ],
        out_specs=[tok(d, 0), tok(d, 0), tok(e_pad, 0)],
        out_shape=[jax.ShapeDtypeStruct((b, t, d), F32), jax.ShapeDtypeStruct((b, t, d), F32),
                   jax.ShapeDtypeStruct((b, t, e_pad), F32)],
        compiler_params=_cparams(("arbitrary", "arbitrary")),
        name="merge_norm_router",
    )(xs, ya_lat, ya_ctx, yb, o_f, o_b, p, p, p, p, gdn_norm_w.reshape(1, C_HEAD_DIM), w_branch, w_out,
      modb, norm2_w.reshape(1, d), modb, modb, rw)


MOE_ROWS = 256


def _routing_tables(scores, router_bias):
    n = scores.shape[0]
    rb = MOE_ROWS
    grp = (scores + router_bias.astype(F32)).reshape(n, N_GROUPS, EXPERTS_PER_GROUP)

    def top2(a):
        ids = lax.broadcasted_iota(jnp.int32, a.shape, a.ndim - 1)
        i1 = jnp.argmax(a, axis=-1)
        v1 = jnp.max(a, axis=-1)
        rest = jnp.where(ids == i1[..., None], -jnp.inf, a)
        i2 = jnp.argmax(rest, axis=-1)
        v2 = jnp.max(rest, axis=-1)
        return (v1, v2), jnp.stack([i1, i2], axis=-1).astype(jnp.int32)

    (g1, g2), _ = top2(grp)
    best = jnp.argmax(g1 + g2, axis=-1)
    in_grp = jnp.take_along_axis(grp, best[:, None, None], axis=1)[:, 0]
    _, local = top2(in_grp)
    expert = best[:, None] * EXPERTS_PER_GROUP + local
    w = jnp.take_along_axis(scores, expert, axis=-1)
    gate = w / jnp.sum(w, axis=-1, keepdims=True)
    lo = jnp.min(local, axis=-1)
    hi = jnp.max(local, axis=-1)
    first_is_lo = local[:, 0] < local[:, 1]
    g_lo = jnp.where(first_is_lo, gate[:, 0], gate[:, 1])
    g_hi = jnp.where(first_is_lo, gate[:, 1], gate[:, 0])
    pair = lo * (2 * EXPERTS_PER_GROUP - lo - 1) // 2 + (hi - lo - 1)
    cls = (best * N_PAIR + pair).astype(jnp.int32)

    blk = LANES
    onehot = (cls[:, None] == jnp.arange(N_CLASS, dtype=jnp.int32)[None, :]).astype(F32).reshape(n // blk, blk, N_CLASS)
    tril = (jnp.arange(blk)[:, None] >= jnp.arange(blk)[None, :]).astype(F32)
    within = jnp.einsum("ij,bjc->bic", tril, onehot)
    blk_tot = within[:, -1, :]
    blk_off = jnp.cumsum(blk_tot, axis=0) - blk_tot
    counts = (blk_off[-1] + blk_tot[-1]).astype(jnp.int32)
    rank = jnp.sum(onehot * (within - 1.0 + blk_off[:, None, :]), axis=-1).reshape(n).astype(jnp.int32)
    padded = (counts + rb - 1) // rb * rb
    pad_end = jnp.cumsum(padded)
    pad_start = (pad_end - padded).astype(F32)
    dest = (jnp.sum(onehot * pad_start[None, None, :], axis=-1).reshape(n).astype(jnp.int32) + rank)
    n_blocks = n // rb + N_CLASS
    n_rows = n_blocks * rb
    row_tok = jnp.zeros((n_rows,), jnp.int32).at[dest].set(jnp.arange(n, dtype=jnp.int32))
    tok_row = dest
    gate_tok = jnp.concatenate([g_lo[:, None], g_hi[:, None], jnp.zeros((n, LANES - 2), F32)], axis=1)
    row_gate = jnp.take(gate_tok, row_tok, axis=0)
    n_used = (pad_end[-1] // rb).astype(jnp.int32)
    blk = jnp.minimum(jnp.arange(n_blocks, dtype=jnp.int32), n_used - 1)
    blk_cls = jnp.minimum(jnp.searchsorted(pad_end, blk * rb, side="right"), N_CLASS - 1).astype(jnp.int32)
    pl_lo, pl_hi = [], []
    for a in range(EXPERTS_PER_GROUP):
        for c in range(a + 1, EXPERTS_PER_GROUP):
            pl_lo.append(a)
            pl_hi.append(c)
    pl_lo = jnp.asarray(pl_lo, jnp.int32)
    pl_hi = jnp.asarray(pl_hi, jnp.int32)
    blk_grp = blk_cls // N_PAIR
    blk_pair = blk_cls % N_PAIR
    e_lo = blk_grp * EXPERTS_PER_GROUP + pl_lo[blk_pair]
    e_hi = blk_grp * EXPERTS_PER_GROUP + pl_hi[blk_pair]
    return e_lo, e_hi, n_used.reshape(1), row_tok, tok_row, row_gate


def _row_copy(idx_ref, pos, r, src_hbm, dst, sem):
    return pltpu.make_async_copy(src_hbm.at[pl.ds(idx_ref[pos], 1), :], dst.at[pl.ds(r, 1), :], sem)


def _start_row_gather(idx_ref, base, n_rows, src_hbm, dst, sem):
    for r in range(n_rows):
        _row_copy(idx_ref, base + r, r, src_hbm, dst, sem).start()


def _wait_row_gather(n_rows, src_hbm, dst, sem):
    pltpu.make_async_copy(src_hbm.at[pl.ds(0, n_rows), :], dst, sem).wait()


def _moe_body(elo_ref, ehi_ref, nused_ref, rowtok_ref, h2_hbm, gate_ref, wg_lo, wu_lo, wd_lo, wg_hi, wu_hi, wd_hi,
              y_ref, xbuf, sem, *, rb, n_blocks):
    i = pl.program_id(0)
    n_used = nused_ref[0]
    slot = i % 2

    @pl.when(i == 0)
    def _():
        _start_row_gather(rowtok_ref, 0, rb, h2_hbm, xbuf.at[0], sem.at[0])

    @pl.when(i <= n_used)
    def _():
        _wait_row_gather(rb, h2_hbm, xbuf.at[slot], sem.at[slot])

    @pl.when(i < n_used)
    def _():
        nxt = jnp.minimum(i + 1, n_blocks - 1)
        _start_row_gather(rowtok_ref, nxt * rb, rb, h2_hbm, xbuf.at[1 - slot], sem.at[1 - slot])
        x = xbuf[slot].astype(BF16)
        gate = gate_ref[...]

        def ffn(wg, wu, wd):
            hg = jnp.dot(x, wg[...], preferred_element_type=F32)
            hu = jnp.dot(x, wu[...], preferred_element_type=F32)
            return jnp.dot((_silu(hg) * hu).astype(BF16), wd[...], preferred_element_type=F32)

        y_ref[...] = gate[:, 0:1] * ffn(wg_lo, wu_lo, wd_lo) + gate[:, 1:2] * ffn(wg_hi, wu_hi, wd_hi)

    @pl.when(i >= n_used)
    def _():
        y_ref[...] = jnp.zeros(y_ref.shape, F32)

    @pl.when(jnp.logical_and(i == n_blocks - 1, i < n_used))
    def _():
        _wait_row_gather(rb, h2_hbm, xbuf.at[1 - slot], sem.at[1 - slot])


def _moe_ffn(h2_flat, tables, w_gate, w_up, w_down):
    e_lo, e_hi, n_used, row_tok, _, row_gate = tables
    n, d = h2_flat.shape
    rb = MOE_ROWS
    n_blocks = e_lo.shape[0]
    de = w_gate.shape[-1]
    w_in_lo = pl.BlockSpec((None, d, de), lambda i, lo, hi, nu, rt: (lo[i], 0, 0))
    w_in_hi = pl.BlockSpec((None, d, de), lambda i, lo, hi, nu, rt: (hi[i], 0, 0))
    w_dn_lo = pl.BlockSpec((None, de, d), lambda i, lo, hi, nu, rt: (lo[i], 0, 0))
    w_dn_hi = pl.BlockSpec((None, de, d), lambda i, lo, hi, nu, rt: (hi[i], 0, 0))
    grid_spec = pltpu.PrefetchScalarGridSpec(
        num_scalar_prefetch=4,
        grid=(n_blocks,),
        in_specs=[
            pl.BlockSpec(memory_space=pl.ANY),
            pl.BlockSpec((rb, LANES), lambda i, lo, hi, nu, rt: (i, 0)),
            w_in_lo, w_in_lo, w_dn_lo, w_in_hi, w_in_hi, w_dn_hi,
        ],
        out_specs=pl.BlockSpec((rb, d), lambda i, lo, hi, nu, rt: (i, 0)),
        scratch_shapes=[pltpu.VMEM((2, rb, d), F32), pltpu.SemaphoreType.DMA((2,))],
    )
    return pl.pallas_call(
        functools.partial(_moe_body, rb=rb, n_blocks=n_blocks),
        grid_spec=grid_spec,
        out_shape=jax.ShapeDtypeStruct((n_blocks * rb, d), F32),
        compiler_params=_cparams(("arbitrary",)),
        name="moe_expert_ffn",
    )(e_lo, e_hi, n_used, row_tok, h2_flat, row_gate, w_gate, w_up, w_down, w_gate, w_up, w_down)


def _combine_body(tokrow_ref, x_ref, gate_ref, y_hbm, o_ref, ybuf, sem, *, tm, s_len, tiles_per_batch, n_steps):
    bi = pl.program_id(0)
    i = pl.program_id(1)
    step = bi * tiles_per_batch + i
    slot = step % 2

    @pl.when(step == 0)
    def _():
        _start_row_gather(tokrow_ref, 0, tm, y_hbm, ybuf.at[0], sem.at[0])

    _wait_row_gather(tm, y_hbm, ybuf.at[slot], sem.at[slot])

    @pl.when(step + 1 < n_steps)
    def _():
        _start_row_gather(tokrow_ref, (step + 1) * tm, tm, y_hbm, ybuf.at[1 - slot], sem.at[1 - slot])

    is_ctx = _rows_are_ctx(i, tm, s_len)
    o_ref[...] = x_ref[...] + _pick_mod(gate_ref, is_ctx) * ybuf[slot]


def _combine(xs, y_sorted, tok_row, modb, s_len):
    b, t, d = xs.shape
    tm = CONV_TM
    grid_spec = pltpu.PrefetchScalarGridSpec(
        num_scalar_prefetch=1,
        grid=(b, t // tm),
        in_specs=[
            pl.BlockSpec((None, tm, d), lambda bi, i, tr: (bi, i, 0)),
            pl.BlockSpec((None, 2, d), lambda bi, i, tr: (bi, 0, 5)),
            pl.BlockSpec(memory_space=pl.ANY),
        ],
        out_specs=pl.BlockSpec((None, tm, d), lambda bi, i, tr: (bi, i, 0)),
        scratch_shapes=[pltpu.VMEM((2, tm, d), F32), pltpu.SemaphoreType.DMA((2,))],
    )
    body = functools.partial(_combine_body, tm=tm, s_len=s_len, tiles_per_batch=t // tm, n_steps=b * (t // tm))
    return pl.pallas_call(
        body,
        grid_spec=grid_spec,
        out_shape=jax.ShapeDtypeStruct((b, t, d), F32),
        compiler_params=_cparams(("arbitrary", "arbitrary")),
        name="moe_combine_residual",
    )(tok_row, xs, modb, y_sorted)


def _rope_tables(s_len, ctx_len):
    t = jnp.arange(s_len, dtype=jnp.int32)
    r = (t // GRID_W).astype(F32)
    col = (t % GRID_W).astype(F32)
    axis_dim = A_HEAD_DIM // 2
    inv_freq = ROPE_THETA ** (-jnp.arange(0, axis_dim, 2, dtype=F32) / axis_dim)
    ang_r = r[:, None] * inv_freq
    ang_c = col[:, None] * inv_freq
    cos = jnp.concatenate([jnp.cos(ang_r), jnp.cos(ang_r), jnp.cos(ang_c), jnp.cos(ang_c)], axis=-1)
    sin = jnp.concatenate([-jnp.sin(ang_r), jnp.sin(ang_r), -jnp.sin(ang_c), jnp.sin(ang_c)], axis=-1)
    reps = A_WIDTH // A_HEAD_DIM
    cos = jnp.concatenate([jnp.tile(cos, (1, reps)), jnp.ones((ctx_len, A_WIDTH), F32)], axis=0)
    sin = jnp.concatenate([jnp.tile(sin, (1, reps)), jnp.zeros((ctx_len, A_WIDTH), F32)], axis=0)
    return cos, sin


def _split_w_in(w_in_l):
    oa, oc, ob = A_COLS, A_COLS + C_COLS, A_COLS + C_COLS + B_COLS
    a = w_in_l[:, :oa]
    c_main = w_in_l[:, oa:oa + 4 * C_WIDTH]
    c_small = w_in_l[:, oa + 4 * C_WIDTH:oc]
    bb = w_in_l[:, oc:ob]
    gates = w_in_l[:, ob:]
    main = jnp.concatenate([gates, a, c_main, bb], axis=1).astype(BF16)
    small = jnp.zeros((w_in_l.shape[0], SMALL_W), F32).at[:, :4 * C_HEADS].set(c_small).astype(BF16)
    return main, small


def kernel(x, c, ctx, c_ctx, w_mod, b_mod, norm1_w, norm2_w, w_in, qn_w, kn_w, lam_p, subln_w, conv_dw_w, conv_dw_b,
           conv_ln_w, conv_ln_b, gdn_conv_w, gdn_a_log, gdn_dt_bias, gdn_norm_w, w_branch, w_out, router_w,
           router_bias, w_gate, w_up, w_down):
    b, s_len, d = x.shape
    ctx_len = ctx.shape[1]
    depth = w_mod.shape[0]
    t = s_len + ctx_len
    assert s_len % CONV_TM == 0 and ctx_len % CONV_TM == 0 and s_len % GRID_W == 0

    xs = jnp.concatenate([x, ctx], axis=1)
    rows = -(-(b + 1) // 8) * 8
    cvec = jnp.zeros((rows, d), F32).at[:b].set(c).at[b].set(c_ctx)
    mod = _modulation(cvec, w_mod, b_mod)
    rope_c, rope_s = _rope_tables(s_len, ctx_len)

    for l in range(depth):
        lam_init = 0.8 - 0.6 * math.exp(-0.3 * l)
        modb = jnp.stack([mod[l, :b], jnp.broadcast_to(mod[l, b], (b, N_MOD * d))], axis=1)
        w_main, w_small = _split_w_in(w_in[l])

        p, small = _in_projection(xs, norm1_w[l], modb, w_main, w_small, s_len)
        qt, k, vt = _attn_prep(p, rope_c, rope_s, qn_w[l], kn_w[l])
        ya_lat, ya_ctx = _diff_attention(lam_p[l], qt, k, vt, subln_w[l], s_len, lam_init)
        yb = _conformer(p, conv_dw_w[l], conv_dw_b[l], conv_ln_w[l], conv_ln_b[l], s_len)
        gq, gk, gv, gb = _gdn_prep(p, small, gdn_conv_w[l], gdn_a_log[l], gdn_dt_bias[l], s_len)
        o_f, o_b = _gdn_scan(gq, gk, gv, gb, s_len)
        xs, h2, scores = _merge(xs, ya_lat, ya_ctx, yb, o_f, o_b, p, gdn_norm_w[l], w_branch[l].astype(BF16),
                                w_out[l].astype(BF16), modb, norm2_w[l], router_w, s_len)

        tables = _routing_tables(scores.reshape(b * t, -1)[:, :N_EXPERTS], router_bias)
        y_sorted = _moe_ffn(h2.reshape(b * t, d), tables, w_gate[l].astype(BF16), w_up[l].astype(BF16),
                            w_down[l].astype(BF16))
        xs = _combine(xs, y_sorted, tables[4], modb, s_len)
    return xs[:, :s_len]
```

```python
import functools
import math

import jax
import jax.numpy as jnp
from jax import lax
from jax.experimental import pallas as pl
from jax.experimental.pallas import tpu as pltpu

F32 = jnp.float32
BF16 = jnp.bfloat16
HIGHEST = lax.Precision.HIGHEST

D_MODEL = 1024
GRID_W = 64
N_MOD = 6
EPS = 1e-6

A_HEADS = 4
A_HEAD_DIM = 64
A_V_DIM = 2 * A_HEAD_DIM
A_WIDTH = A_HEADS * A_V_DIM
ROPE_THETA = 10000.0
SUBLN_EPS = 1e-5

CONV_CH = 512
CONV_K = 31

C_HEADS = 4
C_HEAD_DIM = 128
C_WIDTH = C_HEADS * C_HEAD_DIM
SHORT_K = 5
GDN_CHUNK = 128
NEUMANN_BLK = 64

N_BRANCH = 3
BRANCH_W = 512

A_COLS = 3 * A_WIDTH
C_COLS = 4 * C_WIDTH + 4 * C_HEADS
B_COLS = 2 * CONV_CH
GATE_COLS = N_BRANCH * D_MODEL

N_EXPERTS = 16
N_GROUPS = 4
EXPERTS_PER_GROUP = N_EXPERTS // N_GROUPS
TOP_K = 2
D_EXPERT = 1024
N_PAIR = EXPERTS_PER_GROUP * (EXPERTS_PER_GROUP - 1) // 2
N_CLASS = N_GROUPS * N_PAIR

LANES = 128
HALO = 16
CONV_TM = 256
ROW_CHUNK = 32
VMEM_LIMIT = 52 * 1024 * 1024

P_COLS = GATE_COLS + A_COLS + 4 * C_WIDTH + B_COLS
PB_AQ, PB_AK, PB_AV = 6, 7, 8
PB_CQKV = 3
PB_CZ, PB_BA, PB_BG = 12, 13, 14
SMALL_W = LANES


def _cparams(sem):
    return pltpu.CompilerParams(dimension_semantics=sem, vmem_limit_bytes=VMEM_LIMIT)


def _silu(x):
    return x * jax.nn.sigmoid(x)


def _pick(n, prefs):
    for p in prefs:
        if n % p == 0:
            return p
    raise ValueError(f"no tile in {prefs} divides {n}")


def _mod_body(cv_ref, w_ref, b_ref, o_ref):
    s = _silu(cv_ref[...])
    o_ref[...] = jnp.dot(s, w_ref[...], preferred_element_type=F32, precision=HIGHEST) + b_ref[...]


def _modulation(cvec, w_mod, b_mod):
    depth, d, n = w_mod.shape
    r = cvec.shape[0]
    tn = _pick(n, (1536, 1024, 512, 128))
    return pl.pallas_call(
        _mod_body,
        grid=(depth, n // tn),
        in_specs=[
            pl.BlockSpec((r, d), lambda l, j: (0, 0)),
            pl.BlockSpec((None, d, tn), lambda l, j: (l, 0, j)),
            pl.BlockSpec((None, 1, tn), lambda l, j: (l, 0, j)),
        ],
        out_specs=pl.BlockSpec((None, r, tn), lambda l, j: (l, 0, j)),
        out_shape=jax.ShapeDtypeStruct((depth, r, n), F32),
        compiler_params=_cparams(("arbitrary", "arbitrary")),
        name="modulation",
    )(cvec, w_mod, b_mod.reshape(depth, 1, n))


def _rows_are_ctx(tile_idx, tm, s_len):
    rows = tile_idx * tm + lax.broadcasted_iota(jnp.int32, (tm, 1), 0)
    return rows >= s_len


def _pick_mod(mod_ref, is_ctx):
    return jnp.where(is_ctx, mod_ref[1:2, :], mod_ref[0:1, :])


def _rms_modulate(x, w, shift, scale):
    ms = jnp.mean(x * x, axis=-1, keepdims=True)
    return x * lax.rsqrt(ms + EPS) * w * (1.0 + scale) + shift


def _inproj_body(x_ref, nw_ref, sh_ref, sc_ref, w_ref, ws_ref, p_ref, small_ref, h_scr, *, tm, s_len):
    i = pl.program_id(1)
    j = pl.program_id(2)

    @pl.when(j == 0)
    def _():
        is_ctx = _rows_are_ctx(i, tm, s_len)
        h = _rms_modulate(x_ref[...], nw_ref[...], _pick_mod(sh_ref, is_ctx), _pick_mod(sc_ref, is_ctx))
        h = h.astype(BF16)
        h_scr[...] = h
        small_ref[...] = jnp.dot(h, ws_ref[...], preferred_element_type=F32)

    p_ref[...] = jnp.dot(h_scr[...], w_ref[...], preferred_element_type=F32).astype(BF16)


def _in_projection(xs, norm_w, modb, w_main, w_small, s_len):
    b, t, d = xs.shape
    n = w_main.shape[1]
    tm = _pick(t, (768, 512, 256))
    tn = _pick(n, (3840, 1536, 512))
    body = functools.partial(_inproj_body, tm=tm, s_len=s_len)
    return pl.pallas_call(
        body,
        grid=(b, t // tm, n // tn),
        in_specs=[
            pl.BlockSpec((None, tm, d), lambda bi, i, j: (bi, i, 0)),
            pl.BlockSpec((1, d), lambda bi, i, j: (0, 0)),
            pl.BlockSpec((None, 2, d), lambda bi, i, j: (bi, 0, 0)),
            pl.BlockSpec((None, 2, d), lambda bi, i, j: (bi, 0, 1)),
            pl.BlockSpec((d, tn), lambda bi, i, j: (0, j)),
            pl.BlockSpec((d, SMALL_W), lambda bi, i, j: (0, 0)),
        ],
        out_specs=[
            pl.BlockSpec((None, tm, tn), lambda bi, i, j: (bi, i, j)),
            pl.BlockSpec((None, tm, SMALL_W), lambda bi, i, j: (bi, i, 0)),
        ],
        out_shape=[
            jax.ShapeDtypeStruct((b, t, n), BF16),
            jax.ShapeDtypeStruct((b, t, SMALL_W), F32),
        ],
        scratch_shapes=[pltpu.VMEM((tm, d), BF16)],
        compiler_params=_cparams(("arbitrary", "arbitrary", "arbitrary")),
        name="in_projection",
    )(xs, norm_w.reshape(1, d), modb, modb, w_main, w_small)


def _group_mean(x2, bd):
    hi = x2.astype(BF16)
    lo = (x2 - hi.astype(F32)).astype(BF16)
    return (jnp.dot(hi, bd, preferred_element_type=F32) + jnp.dot(lo, bd, preferred_element_type=F32))


def _rope_partner(y):
    n = y.shape[-1]
    lane = lax.broadcasted_iota(jnp.int32, y.shape, 1)
    return jnp.where((lane & 16) == 0, pltpu.roll(y, n - 16, 1), pltpu.roll(y, 16, 1))


def _attnprep_body(q_ref, k_ref, v_ref, rc_ref, rs_ref, qw_ref, kw_ref, bd_ref, qt_ref, ko_ref, vt_ref):
    bd = bd_ref[...]
    rc = rc_ref[...]
    rs = rs_ref[...]

    def prep(x_ref, w_ref, post):
        x = x_ref[...].astype(F32)
        y = x * lax.rsqrt(_group_mean(x * x, bd) + EPS) * w_ref[...]
        y = y * rc + _rope_partner(y) * rs
        return y * post

    qt_ref[...] = prep(q_ref, qw_ref, A_HEAD_DIM ** -0.5 * math.log2(math.e)).T.astype(BF16)
    ko_ref[...] = prep(k_ref, kw_ref, 1.0).astype(BF16)
    vt_ref[...] = v_ref[...].astype(F32).T.astype(BF16)


def _attn_prep(p, rope_c, rope_s, qn_w, kn_w):
    b, t, _ = p.shape
    w = A_WIDTH
    tm = _pick(t, (768, 512, 256))
    g = jnp.arange(w) // A_HEAD_DIM
    bd = jnp.where(g[:, None] == g[None, :], 1.0 / A_HEAD_DIM, 0.0).astype(BF16)
    reps = w // A_HEAD_DIM
    tok = lambda blk: pl.BlockSpec((None, tm, w), lambda bi, i: (bi, i, blk))
    tr = pl.BlockSpec((None, w, tm), lambda bi, i: (bi, 0, i))
    return pl.pallas_call(
        _attnprep_body,
        grid=(b, t // tm),
        in_specs=[
            tok(PB_AQ), tok(PB_AK), tok(PB_AV),
            pl.BlockSpec((tm, w), lambda bi, i: (i, 0)),
            pl.BlockSpec((tm, w), lambda bi, i: (i, 0)),
            pl.BlockSpec((1, w), lambda bi, i: (0, 0)),
            pl.BlockSpec((1, w), lambda bi, i: (0, 0)),
            pl.BlockSpec((w, w), lambda bi, i: (0, 0)),
        ],
        out_specs=[tr, tok(0), tr],
        out_shape=[jax.ShapeDtypeStruct((b, w, t), BF16), jax.ShapeDtypeStruct((b, t, w), BF16),
                   jax.ShapeDtypeStruct((b, w, t), BF16)],
        compiler_params=_cparams(("arbitrary", "arbitrary")),
        name="attn_prep",
    )(p, p, p, rope_c, rope_s, jnp.tile(qn_w, reps).reshape(1, w), jnp.tile(kn_w, reps).reshape(1, w), bd)


def _attn_body(lam_ref, qt_ref, k_ref, vt_ref, sw_ref, o_ref, m_scr, l_scr, acc_scr, sa_scr, sb_scr,
               *, tq, tk, kv0, nkv, lam_init):
    qt = qt_ref[...]
    row = lax.broadcasted_iota(jnp.int32, qt.shape, 0)
    zero = jnp.zeros_like(qt)
    qm = (jnp.where(row < A_HEAD_DIM, qt, zero), jnp.where(row >= A_HEAD_DIM, qt, zero))
    m_scr[...] = jnp.full(m_scr.shape, -jnp.inf, F32)
    l_scr[...] = jnp.zeros(l_scr.shape, F32)
    acc_scr[...] = jnp.zeros(acc_scr.shape, F32)

    def scores(c, dst):
        off = pl.multiple_of(kv0 + c * tk, tk)
        kc = k_ref[pl.ds(off, tk), :]
        for mi in range(2):
            dst[mi] = jnp.dot(kc, qm[mi], preferred_element_type=F32)

    def accumulate(c, src):
        off = pl.multiple_of(kv0 + c * tk, tk)
        vtc = vt_ref[:, pl.ds(off, tk)]
        s = [src[mi] for mi in range(2)]
        m_old = [m_scr[mi] for mi in range(2)]
        m_new = [jnp.maximum(m_old[mi], jnp.max(s[mi], axis=0, keepdims=True)) for mi in range(2)]
        p = [jnp.exp2(s[mi] - m_new[mi]) for mi in range(2)]
        alpha = [jnp.exp2(m_old[mi] - m_new[mi]) for mi in range(2)]
        for mi in range(2):
            l_scr[mi] = alpha[mi] * l_scr[mi] + jnp.sum(p[mi], axis=0, keepdims=True)
            m_scr[mi] = m_new[mi]
        pv = [jnp.dot(vtc, p[mi].astype(BF16), preferred_element_type=F32) for mi in range(2)]
        for mi in range(2):
            acc_scr[mi] = alpha[mi] * acc_scr[mi] + pv[mi]

    scores(0, sa_scr)
    n_pairs = (nkv - 1) // 2

    def pair(i, carry):
        c = 2 * i
        scores(c + 1, sb_scr)
        accumulate(c, sa_scr)
        scores(c + 2, sa_scr)
        accumulate(c + 1, sb_scr)
        return carry

    lax.fori_loop(0, n_pairs, pair, 0)
    if nkv % 2 == 0:
        scores(nkv - 1, sb_scr)
        accumulate(nkv - 2, sa_scr)
        accumulate(nkv - 1, sb_scr)
    else:
        accumulate(nkv - 1, sa_scr)

    lp = lam_ref[...]
    lam = (jnp.exp(jnp.sum(lp[0:1] * lp[1:2], axis=-1, keepdims=True))
           - jnp.exp(jnp.sum(lp[2:3] * lp[3:4], axis=-1, keepdims=True)) + lam_init)
    o = (acc_scr[0] / l_scr[0] - lam * (acc_scr[1] / l_scr[1])).T
    ms = jnp.mean(o * o, axis=-1, keepdims=True)
    o_ref[...] = (o * lax.rsqrt(ms + SUBLN_EPS) * sw_ref[...] * (1.0 - lam_init)).astype(o_ref.dtype)


def _attention_call(lam_p, qt, k, vt, subln_w, *, q0, nq_rows, tq, kv0, nkv_rows, tk, lam_init, name):
    b, t, _ = k.shape
    hd = A_V_DIM
    assert q0 % tq == 0 and nq_rows % tq == 0 and kv0 % tk == 0 and nkv_rows % tk == 0
    qb0 = q0 // tq
    body = functools.partial(_attn_body, tq=tq, tk=tk, kv0=kv0, nkv=nkv_rows // tk, lam_init=lam_init)
    return pl.pallas_call(
        body,
        grid=(b, A_HEADS, nq_rows // tq),
        in_specs=[
            pl.BlockSpec((4, A_HEAD_DIM), lambda bi, h, i: (0, 0)),
            pl.BlockSpec((None, hd, tq), lambda bi, h, i: (bi, h, qb0 + i)),
            pl.BlockSpec((None, t, hd), lambda bi, h, i: (bi, 0, h)),
            pl.BlockSpec((None, hd, t), lambda bi, h, i: (bi, h, 0)),
            pl.BlockSpec((1, hd), lambda bi, h, i: (0, 0)),
        ],
        out_specs=pl.BlockSpec((None, tq, hd), lambda bi, h, i: (bi, i, h)),
        out_shape=jax.ShapeDtypeStruct((b, nq_rows, A_WIDTH), BF16),
        scratch_shapes=[pltpu.VMEM((2, 1, tq), F32), pltpu.VMEM((2, 1, tq), F32), pltpu.VMEM((2, hd, tq), F32),
                        pltpu.VMEM((2, tk, tq), F32), pltpu.VMEM((2, tk, tq), F32)],
        compiler_params=_cparams(("arbitrary", "arbitrary", "arbitrary")),
        name=name,
    )(lam_p, qt, k, vt, subln_w.reshape(1, hd))


def _diff_attention(lam_p, qt, k, vt, subln_w, s_len, lam_init):
    b, t, _ = k.shape
    ctx_len = t - s_len
    tq = _pick(s_len, (512, 256))
    tk = _pick(t, (1408, 768, 256))
    ya_lat = _attention_call(lam_p, qt, k, vt, subln_w, q0=0, nq_rows=s_len, tq=tq,
                             kv0=0, nkv_rows=t, tk=tk, lam_init=lam_init, name="diff_attention_lat")
    tc = _pick(ctx_len, (256, 128))
    assert s_len % tc == 0
    ya_ctx = _attention_call(lam_p, qt, k, vt, subln_w, q0=s_len, nq_rows=ctx_len, tq=tc,
                             kv0=s_len, nkv_rows=ctx_len, tk=tc, lam_init=lam_init, name="diff_attention_ctx")
    return ya_lat, ya_ctx


def _halo_specs(tm, t, width, col_blk):
    per = tm // HALO
    last = t // HALO - 1
    main = pl.BlockSpec((None, tm, width), lambda bi, i: (bi, i, col_blk))
    prev = pl.BlockSpec((None, HALO, width), lambda bi, i: (bi, jnp.maximum(i * per - 1, 0), col_blk))
    nxt = pl.BlockSpec((None, HALO, width), lambda bi, i: (bi, jnp.minimum((i + 1) * per, last), col_blk))
    return main, prev, nxt


def _halo_ok(i, s_tiles, n_tiles):
    prev_ok = jnp.logical_and(i != 0, i != s_tiles)
    next_ok = jnp.logical_and(i != s_tiles - 1, i != n_tiles - 1)
    return prev_ok, next_ok


def _conv_rows(buf, w_ref, r0, taps, c0, c1):
    base = HALO - taps // 2 + r0
    acc = jnp.zeros((ROW_CHUNK, c1 - c0), F32)
    for j in range(taps):
        acc = acc + buf[base + j:base + j + ROW_CHUNK, c0:c1] * w_ref[j:j + 1, c0:c1]
    return acc


SUBLANES = 8


def _conformer_body(a_ref, ap_ref, an_ref, g_ref, gp_ref, gn_ref, w_ref, b_ref, lw_ref, lb_ref, o_ref, buf, phase,
                    *, tm, s_tiles, n_tiles):
    i = pl.program_id(1)
    prev_ok, next_ok = _halo_ok(i, s_tiles, n_tiles)

    def glu(a, g):
        return a[...].astype(F32) * jax.nn.sigmoid(g[...].astype(F32))

    buf[0:HALO, :] = jnp.where(prev_ok, glu(ap_ref, gp_ref), 0.0)
    buf[HALO:HALO + tm, :] = glu(a_ref, g_ref)
    buf[HALO + tm:2 * HALO + tm, :] = jnp.where(next_ok, glu(an_ref, gn_ref), 0.0)
    n_rows = tm + 2 * HALO - SUBLANES
    for p in range(SUBLANES):
        phase[p, 0:n_rows, :] = buf[p:p + n_rows, :]
    base = HALO - CONV_K // 2
    for r0 in range(0, tm, ROW_CHUNK):
        acc = jnp.zeros((ROW_CHUNK, CONV_CH), F32)
        for j in range(CONV_K):
            o = base + j
            lo = r0 + o - o % SUBLANES
            acc = acc + phase[o % SUBLANES, lo:lo + ROW_CHUNK, :] * w_ref[j:j + 1, :]
        y = acc + b_ref[...]
        mu = jnp.mean(y, axis=-1, keepdims=True)
        yc = y - mu
        var = jnp.mean(yc * yc, axis=-1, keepdims=True)
        z = yc * lax.rsqrt(var + 1e-5) * lw_ref[...] + lb_ref[...]
        o_ref[r0:r0 + ROW_CHUNK, :] = _silu(z).astype(o_ref.dtype)


def _conformer(p, dw_w, dw_b, ln_w, ln_b, s_len):
    b, t, _ = p.shape
    tm = CONV_TM
    ch = CONV_CH
    a_specs = _halo_specs(tm, t, ch, PB_BA)
    g_specs = _halo_specs(tm, t, ch, PB_BG)
    vec = pl.BlockSpec((1, ch), lambda bi, i: (0, 0))
    body = functools.partial(_conformer_body, tm=tm, s_tiles=s_len // tm, n_tiles=t // tm)
    return pl.pallas_call(
        body,
        grid=(b, t // tm),
        in_specs=[*a_specs, *g_specs, pl.BlockSpec((CONV_K, ch), lambda bi, i: (0, 0)), vec, vec, vec],
        out_specs=pl.BlockSpec((None, tm, ch), lambda bi, i: (bi, i, 0)),
        out_shape=jax.ShapeDtypeStruct((b, t, ch), BF16),
        scratch_shapes=[pltpu.VMEM((tm + 2 * HALO, ch), F32), pltpu.VMEM((SUBLANES, tm + 2 * HALO - SUBLANES, ch), F32)],
        compiler_params=_cparams(("arbitrary", "arbitrary")),
        name="conformer_conv",
    )(p, p, p, p, p, p, dw_w, dw_b.reshape(1, ch), ln_w.reshape(1, ch), ln_b.reshape(1, ch))


def _softplus(x):
    return jnp.maximum(x, 0.0) + jnp.log(1.0 + jnp.exp(-jnp.abs(x)))


def _gdnprep_body(x_ref, xp_ref, xn_ref, small_ref, w_ref, alog_ref, dtb_ref, q_ref, k_ref, v_ref, gb_ref, buf,
                  *, tm, s_tiles, n_tiles):
    i = pl.program_id(1)
    prev_ok, next_ok = _halo_ok(i, s_tiles, n_tiles)
    buf[0:HALO, :] = jnp.where(prev_ok, xp_ref[...].astype(F32), 0.0)
    buf[HALO:HALO + tm, :] = x_ref[...].astype(F32)
    buf[HALO + tm:2 * HALO + tm, :] = jnp.where(next_ok, xn_ref[...].astype(F32), 0.0)
    outs = (q_ref, k_ref, v_ref)
    for r0 in range(0, tm, ROW_CHUNK):
        for part in range(3):
            y = _silu(_conv_rows(buf, w_ref, r0, SHORT_K, part * C_WIDTH, (part + 1) * C_WIDTH))
            if part < 2:
                post = C_HEAD_DIM ** -0.5 if part == 0 else 1.0
                for h in range(C_HEADS):
                    yh = y[:, h * C_HEAD_DIM:(h + 1) * C_HEAD_DIM]
                    n = yh * lax.rsqrt(jnp.sum(yh * yh, axis=-1, keepdims=True) + 1e-6)
                    outs[part][r0:r0 + ROW_CHUNK, h * C_HEAD_DIM:(h + 1) * C_HEAD_DIM] = n * post
            else:
                outs[part][r0:r0 + ROW_CHUNK, :] = y
    sm = small_ref[...]
    lane = lax.broadcasted_iota(jnp.int32, sm.shape, 1)
    gdec = -jnp.exp(alog_ref[...]) * _softplus(sm + dtb_ref[...])
    gb_ref[...] = jnp.where(lane < 2 * C_HEADS, jax.nn.sigmoid(sm), gdec)


def _gdn_prep(p, small, conv_w, a_log, dt_bias, s_len):
    b, t, _ = p.shape
    tm = CONV_TM
    w3 = 3 * C_WIDTH
    x_specs = _halo_specs(tm, t, w3, PB_CQKV)
    pad = SMALL_W - 4 * C_HEADS
    alog = jnp.concatenate([jnp.zeros((2 * C_HEADS,), F32), a_log.reshape(-1), jnp.zeros((pad,), F32)]).reshape(1, SMALL_W)
    dtb = jnp.concatenate([jnp.zeros((2 * C_HEADS,), F32), dt_bias.reshape(-1), jnp.zeros((pad,), F32)]).reshape(1, SMALL_W)
    vec = pl.BlockSpec((1, SMALL_W), lambda bi, i: (0, 0))
    out_blk = pl.BlockSpec((None, tm, C_WIDTH), lambda bi, i: (bi, i, 0))
    body = functools.partial(_gdnprep_body, tm=tm, s_tiles=s_len // tm, n_tiles=t // tm)
    return pl.pallas_call(
        body,
        grid=(b, t // tm),
        in_specs=[*x_specs, pl.BlockSpec((None, tm, SMALL_W), lambda bi, i: (bi, i, 0)),
                  pl.BlockSpec((SHORT_K, w3), lambda bi, i: (0, 0)), vec, vec],
        out_specs=[out_blk, out_blk, out_blk, pl.BlockSpec((None, tm, SMALL_W), lambda bi, i: (bi, i, 0))],
        out_shape=[jax.ShapeDtypeStruct((b, t, C_WIDTH), F32)] * 3 + [jax.ShapeDtypeStruct((b, t, SMALL_W), F32)],
        scratch_shapes=[pltpu.VMEM((tm + 2 * HALO, w3), F32)],
        compiler_params=_cparams(("arbitrary", "arbitrary")),
        name="gdn_prep",
    )(p, p, p, small, conv_w, alog, dtb)


def _mm(a, b):
    return jnp.dot(a.astype(BF16), b.astype(BF16), preferred_element_type=F32)


def _mm_nt(a, b):
    return lax.dot_general(a.astype(BF16), b.astype(BF16), (((1,), (1,)), ((), ())), preferred_element_type=F32)


def _mm_tn(a, b):
    return lax.dot_general(a.astype(BF16), b.astype(BF16), (((0,), (0,)), ((), ())), preferred_element_type=F32)


def _gdn_chunks(chains):
    n = len(chains)
    c = chains[0][0].shape[0]
    ii = lax.broadcasted_iota(jnp.int32, (c, c), 0)
    jj = lax.broadcasted_iota(jnp.int32, (c, c), 1)
    eye = (ii == jj).astype(F32)
    incl = [(ii <= jj) if ch[7] else (ii >= jj) for ch in chains]
    strict = [(ii < jj) if ch[7] else (ii > jj) for ch in chains]
    q = [ch[0] for ch in chains]
    k = [ch[1] for ch in chains]
    v = [ch[2] for ch in chains]
    gcol = [ch[3] for ch in chains]
    bcol = [ch[5] for ch in chains]
    state = [ch[6] for ch in chains]
    decay = [jnp.where(incl[i], jnp.exp(jnp.where(incl[i], gcol[i] - chains[i][4], 0.0)), 0.0) for i in range(n)]
    kb = [k[i] * bcol[i] for i in range(n)]
    low = [jnp.where(strict[i], _mm_nt(kb[i], k[i]) * decay[i], 0.0) for i in range(n)]
    intra = [jnp.where(incl[i], _mm_nt(q[i], k[i]) * decay[i], 0.0) for i in range(n)]
    blk = min(c, NEUMANN_BLK)
    assert c in (blk, 2 * blk)
    if c > blk:
        same = (ii // blk) == (jj // blk)
        low_d = [jnp.where(same, low[i], 0.0) for i in range(n)]
        low_o = [jnp.where(same, 0.0, low[i]) for i in range(n)]
    else:
        low_d = low
    pw = [-low_d[i] for i in range(n)]
    t_inv = [eye + pw[i] for i in range(n)]
    for _ in range(int(math.log2(blk)) - 1):
        pw = [_mm(pw[i], pw[i]) for i in range(n)]
        t_inv = [t_inv[i] + _mm(t_inv[i], pw[i]) for i in range(n)]
    if c > blk:
        od = [_mm(low_o[i], t_inv[i]) for i in range(n)]
        t_inv = [t_inv[i] - _mm(t_inv[i], od[i]) for i in range(n)]
    eg = [jnp.exp(gcol[i]) for i in range(n)]
    u = [_mm(t_inv[i], v[i] * bcol[i]) for i in range(n)]
    w = [_mm(t_inv[i], kb[i] * eg[i]) for i in range(n)]
    g_last = [gcol[i][0:1, :] if chains[i][7] else gcol[i][c - 1:c, :] for i in range(n)]
    k_dec = [k[i] * jnp.exp(g_last[i] - gcol[i]) for i in range(n)]
    q_dec = [q[i] * eg[i] for i in range(n)]
    v_new = [u[i] - _mm(w[i], state[i]) for i in range(n)]
    o = [_mm(q_dec[i], state[i]) + _mm(intra[i], v_new[i]) for i in range(n)]
    new_state = [state[i] * jnp.exp(g_last[i]) + _mm_tn(k_dec[i], v_new[i]) for i in range(n)]
    return list(zip(o, new_state))


def _gdn_body(qf_ref, kf_ref, vf_ref, gf_ref, qb_ref, kb_ref, vb_ref, gbb_ref, of_ref, ob_ref, st_scr):
    @pl.when(pl.program_id(1) == 0)
    def _():
        st_scr[...] = jnp.zeros(st_scr.shape, F32)

    c = GDN_CHUNK
    ii = lax.broadcasted_iota(jnp.int32, (c, c), 0)
    jj = lax.broadcasted_iota(jnp.int32, (c, c), 1)
    dirs = ((qf_ref, kf_ref, vf_ref, gf_ref, of_ref, False), (qb_ref, kb_ref, vb_ref, gbb_ref, ob_ref, True))
    chains = []
    for d, (q_ref, k_ref, v_ref, g_ref, o_ref, rev) in enumerate(dirs):
        gb = g_ref[...]
        tri = ((ii <= jj) if rev else (ii >= jj)).astype(F32)
        gc = jnp.dot(tri, gb, preferred_element_type=F32, precision=HIGHEST)
        gct = gc.T
        for h in range(C_HEADS):
            sl = slice(h * C_HEAD_DIM, (h + 1) * C_HEAD_DIM)
            bc = d * C_HEADS + h
            gcn = 2 * C_HEADS + bc
            chains.append((q_ref[:, sl], k_ref[:, sl], v_ref[:, sl], gc[:, gcn:gcn + 1], gct[gcn:gcn + 1, :],
                           gb[:, bc:bc + 1], st_scr[bc], rev))
    results = _gdn_chunks(chains)
    for d, (_, _, _, _, o_ref, _) in enumerate(dirs):
        for h in range(C_HEADS):
            bc = d * C_HEADS + h
            o, st = results[bc]
            o_ref[:, h * C_HEAD_DIM:(h + 1) * C_HEAD_DIM] = o
            st_scr[bc] = st


def _gdn_scan(q, k, v, gb, s_len):
    b, t, w = q.shape
    c = GDN_CHUNK
    n_lat = s_len // c
    n_all = t // c
    n_ctx = n_all - n_lat

    def fwd(bi, s):
        return (bi, jnp.where(s < n_ctx, n_lat + s, s - n_ctx), 0)

    def bwd(bi, s):
        return (bi, jnp.where(s < n_ctx, n_all - 1 - s, n_all - 1 - s), 0)

    wide_f = pl.BlockSpec((None, c, w), fwd)
    wide_b = pl.BlockSpec((None, c, w), bwd)
    nar_f = pl.BlockSpec((None, c, SMALL_W), fwd)
    nar_b = pl.BlockSpec((None, c, SMALL_W), bwd)
    return pl.pallas_call(
        _gdn_body,
        grid=(b, n_all),
        in_specs=[wide_f, wide_f, wide_f, nar_f, wide_b, wide_b, wide_b, nar_b],
        out_specs=[wide_f, wide_b],
        out_shape=[jax.ShapeDtypeStruct((b, t, w), F32)] * 2,
        scratch_shapes=[pltpu.VMEM((2 * C_HEADS, C_HEAD_DIM, C_HEAD_DIM), F32)],
        compiler_params=_cparams(("arbitrary", "arbitrary")),
        name="gdn_scan",
    )(q, k, v, gb, q, k, v, gb)


def _merge_body(x_ref, yal_ref, yac_ref, yb_ref, of_ref, ob_ref, z_ref, g0_ref, g1_ref, g2_ref, gnw_ref, wb_ref,
                wo_ref, gate_ref, n2_ref, sh_ref, sc_ref, rw_ref, xo_ref, h2_ref, sco_ref, *, tm, s_len):
    i = pl.program_id(1)
    is_ctx = _rows_are_ctx(i, tm, s_len)
    ya = jnp.where(i * tm >= s_len, yac_ref[...], yal_ref[...])
    o = of_ref[...] + ob_ref[...]
    z = z_ref[...].astype(F32)
    parts = []
    for h in range(C_HEADS):
        sl = slice(h * C_HEAD_DIM, (h + 1) * C_HEAD_DIM)
        oh = o[:, sl]
        ms = jnp.mean(oh * oh, axis=-1, keepdims=True)
        parts.append((oh * lax.rsqrt(ms + EPS) * gnw_ref[...] * _silu(z[:, sl])).astype(BF16))
    yc = jnp.concatenate(parts, axis=-1)
    m = jnp.zeros((tm, D_MODEL), F32)
    for bi, (y, g_ref) in enumerate(((ya, g0_ref), (yb_ref[...], g1_ref), (yc, g2_ref))):
        m = m + jax.nn.sigmoid(g_ref[...].astype(F32)) * jnp.dot(y, wb_ref[bi], preferred_element_type=F32)
    upd = jnp.dot(m.astype(BF16), wo_ref[...], preferred_element_type=F32)
    x = x_ref[...] + _pick_mod(gate_ref, is_ctx) * upd
    xo_ref[...] = x
    h2 = _rms_modulate(x, n2_ref[...], _pick_mod(sh_ref, is_ctx), _pick_mod(sc_ref, is_ctx))
    h2_ref[...] = h2
    sco_ref[...] = jax.nn.sigmoid(jnp.dot(h2, rw_ref[...], preferred_element_type=F32, precision=HIGHEST))


def _merge(xs, ya_lat, ya_ctx, yb, o_f, o_b, p, gdn_norm_w, w_branch, w_out, modb, norm2_w, router_w, s_len):
    b, t, d = xs.shape
    tm = CONV_TM
    bw = BRANCH_W
    e_pad = LANES
    s_tiles = s_len // tm
    n_tiles = t // tm
    lat_spec = pl.BlockSpec((None, tm, bw), lambda bi, i: (bi, jnp.minimum(i, s_tiles - 1), 0))
    ctx_spec = pl.BlockSpec((None, tm, bw), lambda bi, i: (bi, jnp.maximum(i - s_tiles, 0), 0))
    rw = jnp.zeros((d, e_pad), F32).at[:, :N_EXPERTS].set(router_w)
    tok = lambda width, blk: pl.BlockSpec((None, tm, width), lambda bi, i: (bi, i, blk))
    modspec = lambda k: pl.BlockSpec((None, 2, d), lambda bi, i: (bi, 0, k))
    full = lambda shape: pl.BlockSpec(shape, lambda bi, i: (0,) * len(shape))
    body = functools.partial(_merge_body, tm=tm, s_len=s_len)
    return pl.pallas_call(
        body,
        grid=(b, t // tm),
        in_specs=[
            tok(d, 0), lat_spec, ctx_spec, tok(bw, 0), tok(bw, 0), tok(bw, 0), tok(bw, PB_CZ),
            tok(d, 0), tok(d, 1), tok(d, 2),
            full((1, C_HEAD_DIM)), full((N_BRANCH, bw, d)), full((d, d)),
            modspec(2), full((1, d)), modspec(3), modspec(4), full((d, e_pad)),
        ],
        out_specs=[tok(d, 0), tok(d, 0), tok(e_pad, 0)],
        out_shape=[jax.ShapeDtypeStruct((b, t, d), F32), jax.ShapeDtypeStruct((b, t, d), F32),
                   jax.ShapeDtypeStruct((b, t, e_pad), F32)],
        compiler_params=_cparams(("arbitrary", "arbitrary")),
        name="merge_norm_router",
    )(xs, ya_lat, ya_ctx, yb, o_f, o_b, p, p, p, p, gdn_norm_w.reshape(1, C_HEAD_DIM), w_branch, w_out,
      modb, norm2_w.reshape(1, d), modb, modb, rw)


MOE_ROWS = 256


def _routing_tables(scores, router_bias):
    n = scores.shape[0]
    rb = MOE_ROWS
    grp = (scores + router_bias.astype(F32)).reshape(n, N_GROUPS, EXPERTS_PER_GROUP)

    def top2(a):
        ids = lax.broadcasted_iota(jnp.int32, a.shape, a.ndim - 1)
        i1 = jnp.argmax(a, axis=-1)
        v1 = jnp.max(a, axis=-1)
        rest = jnp.where(ids == i1[..., None], -jnp.inf, a)
        i2 = jnp.argmax(rest, axis=-1)
        v2 = jnp.max(rest, axis=-1)
        return (v1, v2), jnp.stack([i1, i2], axis=-1).astype(jnp.int32)

    (g1, g2), _ = top2(grp)
    best = jnp.argmax(g1 + g2, axis=-1)
    in_grp = jnp.take_along_axis(grp, best[:, None, None], axis=1)[:, 0]
    _, local = top2(in_grp)
    expert = best[:, None] * EXPERTS_PER_GROUP + local
    w = jnp.take_along_axis(scores, expert, axis=-1)
    gate = w / jnp.sum(w, axis=-1, keepdims=True)
    lo = jnp.min(local, axis=-1)
    hi = jnp.max(local, axis=-1)
    first_is_lo = local[:, 0] < local[:, 1]
    g_lo = jnp.where(first_is_lo, gate[:, 0], gate[:, 1])
    g_hi = jnp.where(first_is_lo, gate[:, 1], gate[:, 0])
    pair = lo * (2 * EXPERTS_PER_GROUP - lo - 1) // 2 + (hi - lo - 1)
    cls = (best * N_PAIR + pair).astype(jnp.int32)

    blk = LANES
    onehot = (cls[:, None] == jnp.arange(N_CLASS, dtype=jnp.int32)[None, :]).astype(F32).reshape(n // blk, blk, N_CLASS)
    tril = (jnp.arange(blk)[:, None] >= jnp.arange(blk)[None, :]).astype(F32)
    within = jnp.einsum("ij,bjc->bic", tril, onehot)
    blk_tot = within[:, -1, :]
    nb = n // blk
    before = (jnp.arange(nb)[:, None] > jnp.arange(nb)[None, :]).astype(F32)
    blk_off = jnp.dot(before, blk_tot, precision=HIGHEST)
    counts = (blk_off[-1] + blk_tot[-1]).astype(jnp.int32)
    rank = jnp.sum(onehot * (within - 1.0 + blk_off[:, None, :]), axis=-1).reshape(n).astype(jnp.int32)
    padded = (counts + rb - 1) // rb * rb
    upto = jnp.arange(N_CLASS)[:, None] >= jnp.arange(N_CLASS)[None, :]
    pad_end = jnp.sum(jnp.where(upto, padded[None, :], 0), axis=1)
    pad_start = (pad_end - padded).astype(F32)
    dest = (jnp.sum(onehot * pad_start[None, None, :], axis=-1).reshape(n).astype(jnp.int32) + rank)
    n_blocks = n // rb + N_CLASS
    n_rows = n_blocks * rb
    row_tok = jnp.zeros((n_rows,), jnp.int32).at[dest].set(jnp.arange(n, dtype=jnp.int32))
    tok_row = dest
    gate_tok = jnp.concatenate([g_lo[:, None], g_hi[:, None], jnp.zeros((n, LANES - 2), F32)], axis=1)
    n_used = (pad_end[-1] // rb).astype(jnp.int32)
    blk = jnp.minimum(jnp.arange(n_blocks, dtype=jnp.int32), n_used - 1)
    blk_cls = jnp.minimum(jnp.searchsorted(pad_end, blk * rb, side="right"), N_CLASS - 1).astype(jnp.int32)
    pl_lo, pl_hi = [], []
    for a in range(EXPERTS_PER_GROUP):
        for c in range(a + 1, EXPERTS_PER_GROUP):
            pl_lo.append(a)
            pl_hi.append(c)
    pl_lo = jnp.asarray(pl_lo, jnp.int32)
    pl_hi = jnp.asarray(pl_hi, jnp.int32)
    blk_grp = blk_cls // N_PAIR
    blk_pair = blk_cls % N_PAIR
    e_lo = blk_grp * EXPERTS_PER_GROUP + pl_lo[blk_pair]
    e_hi = blk_grp * EXPERTS_PER_GROUP + pl_hi[blk_pair]
    return e_lo, e_hi, n_used.reshape(1), row_tok, tok_row, gate_tok


def _row_copy(idx_ref, pos, r, src_hbm, dst, sem):
    return pltpu.make_async_copy(src_hbm.at[pl.ds(idx_ref[pos], 1), :], dst.at[pl.ds(r, 1), :], sem)


def _start_row_gather(idx_ref, base, n_rows, src_hbm, dst, sem):
    for r in range(n_rows):
        _row_copy(idx_ref, base + r, r, src_hbm, dst, sem).start()


def _wait_row_gather(n_rows, src_hbm, dst, sem):
    pltpu.make_async_copy(src_hbm.at[pl.ds(0, n_rows), :], dst, sem).wait()


def _moe_body(elo_ref, ehi_ref, nused_ref, rowtok_ref, h2_hbm, wg_lo, wu_lo, wd_lo, wg_hi, wu_hi, wd_hi,
              y_ref, xbuf, sem, *, rb, n_blocks):
    i = pl.program_id(0)
    n_used = nused_ref[0]
    slot = i % 2

    @pl.when(i == 0)
    def _():
        _start_row_gather(rowtok_ref, 0, rb, h2_hbm, xbuf.at[0], sem.at[0])

    @pl.when(i <= n_used)
    def _():
        _wait_row_gather(rb, h2_hbm, xbuf.at[slot], sem.at[slot])

    @pl.when(i < n_used)
    def _():
        nxt = jnp.minimum(i + 1, n_blocks - 1)
        _start_row_gather(rowtok_ref, nxt * rb, rb, h2_hbm, xbuf.at[1 - slot], sem.at[1 - slot])
        x = xbuf[slot].astype(BF16)
        d = x.shape[-1]

        def ffn(wg, wu, wd):
            hg = jnp.dot(x, wg[...], preferred_element_type=F32)
            hu = jnp.dot(x, wu[...], preferred_element_type=F32)
            return jnp.dot((_silu(hg) * hu).astype(BF16), wd[...], preferred_element_type=F32)

        y_ref[:, :d] = ffn(wg_lo, wu_lo, wd_lo)
        y_ref[:, d:] = ffn(wg_hi, wu_hi, wd_hi)

    @pl.when(i >= n_used)
    def _():
        y_ref[...] = jnp.zeros(y_ref.shape, F32)

    @pl.when(jnp.logical_and(i == n_blocks - 1, i < n_used))
    def _():
        _wait_row_gather(rb, h2_hbm, xbuf.at[1 - slot], sem.at[1 - slot])


def _moe_ffn(h2_flat, tables, w_gate, w_up, w_down):
    e_lo, e_hi, n_used, row_tok, _, _ = tables
    n, d = h2_flat.shape
    rb = MOE_ROWS
    n_blocks = e_lo.shape[0]
    de = w_gate.shape[-1]
    w_in_lo = pl.BlockSpec((None, d, de), lambda i, lo, hi, nu, rt: (lo[i], 0, 0))
    w_in_hi = pl.BlockSpec((None, d, de), lambda i, lo, hi, nu, rt: (hi[i], 0, 0))
    w_dn_lo = pl.BlockSpec((None, de, d), lambda i, lo, hi, nu, rt: (lo[i], 0, 0))
    w_dn_hi = pl.BlockSpec((None, de, d), lambda i, lo, hi, nu, rt: (hi[i], 0, 0))
    grid_spec = pltpu.PrefetchScalarGridSpec(
        num_scalar_prefetch=4,
        grid=(n_blocks,),
        in_specs=[
            pl.BlockSpec(memory_space=pl.ANY),
            w_in_lo, w_in_lo, w_dn_lo, w_in_hi, w_in_hi, w_dn_hi,
        ],
        out_specs=pl.BlockSpec((rb, 2 * d), lambda i, lo, hi, nu, rt: (i, 0)),
        scratch_shapes=[pltpu.VMEM((2, rb, d), F32), pltpu.SemaphoreType.DMA((2,))],
    )
    return pl.pallas_call(
        functools.partial(_moe_body, rb=rb, n_blocks=n_blocks),
        grid_spec=grid_spec,
        out_shape=jax.ShapeDtypeStruct((n_blocks * rb, 2 * d), F32),
        compiler_params=_cparams(("arbitrary",)),
        name="moe_expert_ffn",
    )(e_lo, e_hi, n_used, row_tok, h2_flat, w_gate, w_up, w_down, w_gate, w_up, w_down)


def _combine_body(tokrow_ref, x_ref, gate_ref, eg_ref, y_hbm, o_ref, ybuf, sem, *, tm, s_len, tiles_per_batch, n_steps):
    bi = pl.program_id(0)
    i = pl.program_id(1)
    step = bi * tiles_per_batch + i
    slot = step % 2

    @pl.when(step == 0)
    def _():
        _start_row_gather(tokrow_ref, 0, tm, y_hbm, ybuf.at[0], sem.at[0])

    _wait_row_gather(tm, y_hbm, ybuf.at[slot], sem.at[slot])

    @pl.when(step + 1 < n_steps)
    def _():
        _start_row_gather(tokrow_ref, (step + 1) * tm, tm, y_hbm, ybuf.at[1 - slot], sem.at[1 - slot])

    is_ctx = _rows_are_ctx(i, tm, s_len)
    d = x_ref.shape[-1]
    eg = eg_ref[...]
    y = ybuf[slot]
    f = eg[:, 0:1] * y[:, :d] + eg[:, 1:2] * y[:, d:]
    o_ref[...] = x_ref[...] + _pick_mod(gate_ref, is_ctx) * f


def _combine(xs, y_sorted, tok_row, gate_tok, modb, s_len):
    b, t, d = xs.shape
    tm = CONV_TM
    grid_spec = pltpu.PrefetchScalarGridSpec(
        num_scalar_prefetch=1,
        grid=(b, t // tm),
        in_specs=[
            pl.BlockSpec((None, tm, d), lambda bi, i, tr: (bi, i, 0)),
            pl.BlockSpec((None, 2, d), lambda bi, i, tr: (bi, 0, 5)),
            pl.BlockSpec((None, tm, LANES), lambda bi, i, tr: (bi, i, 0)),
            pl.BlockSpec(memory_space=pl.ANY),
        ],
        out_specs=pl.BlockSpec((None, tm, d), lambda bi, i, tr: (bi, i, 0)),
        scratch_shapes=[pltpu.VMEM((2, tm, 2 * d), F32), pltpu.SemaphoreType.DMA((2,))],
    )
    body = functools.partial(_combine_body, tm=tm, s_len=s_len, tiles_per_batch=t // tm, n_steps=b * (t // tm))
    return pl.pallas_call(
        body,
        grid_spec=grid_spec,
        out_shape=jax.ShapeDtypeStruct((b, t, d), F32),
        compiler_params=_cparams(("arbitrary", "arbitrary")),
        name="moe_combine_residual",
    )(tok_row, xs, modb, gate_tok.reshape(b, t, LANES), y_sorted)


def _rope_tables(s_len, ctx_len):
    t = jnp.arange(s_len, dtype=jnp.int32)
    r = (t // GRID_W).astype(F32)
    col = (t % GRID_W).astype(F32)
    axis_dim = A_HEAD_DIM // 2
    inv_freq = ROPE_THETA ** (-jnp.arange(0, axis_dim, 2, dtype=F32) / axis_dim)
    ang_r = r[:, None] * inv_freq
    ang_c = col[:, None] * inv_freq
    cos = jnp.concatenate([jnp.cos(ang_r), jnp.cos(ang_r), jnp.cos(ang_c), jnp.cos(ang_c)], axis=-1)
    sin = jnp.concatenate([-jnp.sin(ang_r), jnp.sin(ang_r), -jnp.sin(ang_c), jnp.sin(ang_c)], axis=-1)
    reps = A_WIDTH // A_HEAD_DIM
    cos = jnp.concatenate([jnp.tile(cos, (1, reps)), jnp.ones((ctx_len, A_WIDTH), F32)], axis=0)
    sin = jnp.concatenate([jnp.tile(sin, (1, reps)), jnp.zeros((ctx_len, A_WIDTH), F32)], axis=0)
    return cos, sin


def _split_w_in(w_in_l):
    oa, oc, ob = A_COLS, A_COLS + C_COLS, A_COLS + C_COLS + B_COLS
    a = w_in_l[:, :oa]
    c_main = w_in_l[:, oa:oa + 4 * C_WIDTH]
    c_small = w_in_l[:, oa + 4 * C_WIDTH:oc]
    bb = w_in_l[:, oc:ob]
    gates = w_in_l[:, ob:]
    main = jnp.concatenate([gates, a, c_main, bb], axis=1).astype(BF16)
    small = jnp.zeros((w_in_l.shape[0], SMALL_W), F32).at[:, :4 * C_HEADS].set(c_small).astype(BF16)
    return main, small


def kernel(x, c, ctx, c_ctx, w_mod, b_mod, norm1_w, norm2_w, w_in, qn_w, kn_w, lam_p, subln_w, conv_dw_w, conv_dw_b,
           conv_ln_w, conv_ln_b, gdn_conv_w, gdn_a_log, gdn_dt_bias, gdn_norm_w, w_branch, w_out, router_w,
           router_bias, w_gate, w_up, w_down):
    b, s_len, d = x.shape
    ctx_len = ctx.shape[1]
    depth = w_mod.shape[0]
    t = s_len + ctx_len
    assert s_len % CONV_TM == 0 and ctx_len % CONV_TM == 0 and s_len % GRID_W == 0

    xs = jnp.concatenate([x, ctx], axis=1)
    rows = -(-(b + 1) // 8) * 8
    cvec = jnp.zeros((rows, d), F32).at[:b].set(c).at[b].set(c_ctx)
    mod = _modulation(cvec, w_mod, b_mod)
    rope_c, rope_s = _rope_tables(s_len, ctx_len)

    for l in range(depth):
        lam_init = 0.8 - 0.6 * math.exp(-0.3 * l)
        modb = jnp.stack([mod[l, :b], jnp.broadcast_to(mod[l, b], (b, N_MOD * d))], axis=1)
        w_main, w_small = _split_w_in(w_in[l])

        p, small = _in_projection(xs, norm1_w[l], modb, w_main, w_small, s_len)
        qt, k, vt = _attn_prep(p, rope_c, rope_s, qn_w[l], kn_w[l])
        ya_lat, ya_ctx = _diff_attention(lam_p[l], qt, k, vt, subln_w[l], s_len, lam_init)
        yb = _conformer(p, conv_dw_w[l], conv_dw_b[l], conv_ln_w[l], conv_ln_b[l], s_len)
        gq, gk, gv, gb = _gdn_prep(p, small, gdn_conv_w[l], gdn_a_log[l], gdn_dt_bias[l], s_len)
        o_f, o_b = _gdn_scan(gq, gk, gv, gb, s_len)
        xs, h2, scores = _merge(xs, ya_lat, ya_ctx, yb, o_f, o_b, p, gdn_norm_w[l], w_branch[l].astype(BF16),
                                w_out[l].astype(BF16), modb, norm2_w[l], router_w, s_len)

        tables = _routing_tables(scores.reshape(b * t, -1)[:, :N_EXPERTS], router_bias)
        y_sorted = _moe_ffn(h2.reshape(b * t, d), tables, w_gate[l].astype(BF16), w_up[l].astype(BF16),
                            w_down[l].astype(BF16))
        xs = _combine(xs, y_sorted, tables[4], tables[5], modb, s_len)
    return xs[:, :s_len]
```

```python
import functools
import math

import jax
import jax.numpy as jnp
from jax import lax
from jax.experimental import pallas as pl
from jax.experimental.pallas import tpu as pltpu

F32 = jnp.float32
BF16 = jnp.bfloat16
HIGHEST = lax.Precision.HIGHEST

D_MODEL = 1024
GRID_W = 64
N_MOD = 6
EPS = 1e-6

A_HEADS = 4
A_HEAD_DIM = 64
A_V_DIM = 2 * A_HEAD_DIM
A_WIDTH = A_HEADS * A_V_DIM
ROPE_THETA = 10000.0
SUBLN_EPS = 1e-5

CONV_CH = 512
CONV_K = 31

C_HEADS = 4
C_HEAD_DIM = 128
C_WIDTH = C_HEADS * C_HEAD_DIM
SHORT_K = 5
GDN_CHUNK = 128
NEUMANN_BLK = 64

N_BRANCH = 3
BRANCH_W = 512

A_COLS = 3 * A_WIDTH
C_COLS = 4 * C_WIDTH + 4 * C_HEADS
B_COLS = 2 * CONV_CH
GATE_COLS = N_BRANCH * D_MODEL

N_EXPERTS = 16
N_GROUPS = 4
EXPERTS_PER_GROUP = N_EXPERTS // N_GROUPS
TOP_K = 2
D_EXPERT = 1024
N_PAIR = EXPERTS_PER_GROUP * (EXPERTS_PER_GROUP - 1) // 2
N_CLASS = N_GROUPS * N_PAIR

LANES = 128
HALO = 16
CONV_TM = 256
ROW_CHUNK = 32
VMEM_LIMIT = 52 * 1024 * 1024

P_COLS = GATE_COLS + A_COLS + 4 * C_WIDTH + B_COLS
PB_AQ, PB_AK, PB_AV = 6, 7, 8
PB_CQKV = 3
PB_CZ, PB_BA, PB_BG = 12, 13, 14
SMALL_W = LANES


def _cparams(sem):
    return pltpu.CompilerParams(dimension_semantics=sem, vmem_limit_bytes=VMEM_LIMIT)


def _silu(x):
    return x * jax.nn.sigmoid(x)


def _pick(n, prefs):
    for p in prefs:
        if n % p == 0:
            return p
    raise ValueError(f"no tile in {prefs} divides {n}")


def _mod_body(cv_ref, w_ref, b_ref, o_ref):
    s = _silu(cv_ref[...])
    o_ref[...] = jnp.dot(s, w_ref[...], preferred_element_type=F32, precision=HIGHEST) + b_ref[...]


def _modulation(cvec, w_mod, b_mod):
    depth, d, n = w_mod.shape
    r = cvec.shape[0]
    tn = _pick(n, (1536, 1024, 512, 128))
    return pl.pallas_call(
        _mod_body,
        grid=(depth, n // tn),
        in_specs=[
            pl.BlockSpec((r, d), lambda l, j: (0, 0)),
            pl.BlockSpec((None, d, tn), lambda l, j: (l, 0, j)),
            pl.BlockSpec((None, 1, tn), lambda l, j: (l, 0, j)),
        ],
        out_specs=pl.BlockSpec((None, r, tn), lambda l, j: (l, 0, j)),
        out_shape=jax.ShapeDtypeStruct((depth, r, n), F32),
        compiler_params=_cparams(("arbitrary", "arbitrary")),
        name="modulation",
    )(cvec, w_mod, b_mod.reshape(depth, 1, n))


def _rows_are_ctx(tile_idx, tm, s_len):
    rows = tile_idx * tm + lax.broadcasted_iota(jnp.int32, (tm, 1), 0)
    return rows >= s_len


def _pick_mod(mod_ref, is_ctx):
    return jnp.where(is_ctx, mod_ref[1:2, :], mod_ref[0:1, :])


def _rms_modulate(x, w, shift, scale):
    ms = jnp.mean(x * x, axis=-1, keepdims=True)
    return x * lax.rsqrt(ms + EPS) * w * (1.0 + scale) + shift


def _inproj_body(x_ref, nw_ref, sh_ref, sc_ref, w_ref, ws_ref, p_ref, small_ref, h_scr, *, tm, s_len):
    i = pl.program_id(1)
    j = pl.program_id(2)

    @pl.when(j == 0)
    def _():
        is_ctx = _rows_are_ctx(i, tm, s_len)
        h = _rms_modulate(x_ref[...], nw_ref[...], _pick_mod(sh_ref, is_ctx), _pick_mod(sc_ref, is_ctx))
        h = h.astype(BF16)
        h_scr[...] = h
        small_ref[...] = jnp.dot(h, ws_ref[...], preferred_element_type=F32)

    p_ref[...] = jnp.dot(h_scr[...], w_ref[...], preferred_element_type=F32).astype(BF16)


def _in_projection(xs, norm_w, modb, w_main, w_small, s_len):
    b, t, d = xs.shape
    n = w_main.shape[1]
    tm = _pick(t, (768, 512, 256))
    tn = _pick(n, (3840, 1536, 512))
    body = functools.partial(_inproj_body, tm=tm, s_len=s_len)
    return pl.pallas_call(
        body,
        grid=(b, t // tm, n // tn),
        in_specs=[
            pl.BlockSpec((None, tm, d), lambda bi, i, j: (bi, i, 0)),
            pl.BlockSpec((1, d), lambda bi, i, j: (0, 0)),
            pl.BlockSpec((None, 2, d), lambda bi, i, j: (bi, 0, 0)),
            pl.BlockSpec((None, 2, d), lambda bi, i, j: (bi, 0, 1)),
            pl.BlockSpec((d, tn), lambda bi, i, j: (0, j)),
            pl.BlockSpec((d, SMALL_W), lambda bi, i, j: (0, 0)),
        ],
        out_specs=[
            pl.BlockSpec((None, tm, tn), lambda bi, i, j: (bi, i, j)),
            pl.BlockSpec((None, tm, SMALL_W), lambda bi, i, j: (bi, i, 0)),
        ],
        out_shape=[
            jax.ShapeDtypeStruct((b, t, n), BF16),
            jax.ShapeDtypeStruct((b, t, SMALL_W), F32),
        ],
        scratch_shapes=[pltpu.VMEM((tm, d), BF16)],
        compiler_params=_cparams(("arbitrary", "arbitrary", "arbitrary")),
        name="in_projection",
    )(xs, norm_w.reshape(1, d), modb, modb, w_main, w_small)


def _group_mean(x2, bd):
    hi = x2.astype(BF16)
    lo = (x2 - hi.astype(F32)).astype(BF16)
    return (jnp.dot(hi, bd, preferred_element_type=F32) + jnp.dot(lo, bd, preferred_element_type=F32))


def _rope_partner(y):
    n = y.shape[-1]
    lane = lax.broadcasted_iota(jnp.int32, y.shape, 1)
    return jnp.where((lane & 16) == 0, pltpu.roll(y, n - 16, 1), pltpu.roll(y, 16, 1))


def _attnprep_body(q_ref, k_ref, v_ref, rc_ref, rs_ref, qw_ref, kw_ref, bd_ref, qt_ref, ko_ref, vt_ref):
    bd = bd_ref[...]
    rc = rc_ref[...]
    rs = rs_ref[...]

    def prep(x_ref, w_ref, post):
        x = x_ref[...].astype(F32)
        y = x * lax.rsqrt(_group_mean(x * x, bd) + EPS) * w_ref[...]
        y = y * rc + _rope_partner(y) * rs
        return y * post

    qt_ref[...] = prep(q_ref, qw_ref, A_HEAD_DIM ** -0.5 * math.log2(math.e)).T.astype(BF16)
    ko_ref[...] = prep(k_ref, kw_ref, 1.0).astype(BF16)
    vt_ref[...] = v_ref[...].astype(F32).T.astype(BF16)


def _attn_prep(p, rope_c, rope_s, qn_w, kn_w):
    b, t, _ = p.shape
    w = A_WIDTH
    tm = _pick(t, (768, 512, 256))
    g = jnp.arange(w) // A_HEAD_DIM
    bd = jnp.where(g[:, None] == g[None, :], 1.0 / A_HEAD_DIM, 0.0).astype(BF16)
    reps = w // A_HEAD_DIM
    tok = lambda blk: pl.BlockSpec((None, tm, w), lambda bi, i: (bi, i, blk))
    tr = pl.BlockSpec((None, w, tm), lambda bi, i: (bi, 0, i))
    return pl.pallas_call(
        _attnprep_body,
        grid=(b, t // tm),
        in_specs=[
            tok(PB_AQ), tok(PB_AK), tok(PB_AV),
            pl.BlockSpec((tm, w), lambda bi, i: (i, 0)),
            pl.BlockSpec((tm, w), lambda bi, i: (i, 0)),
            pl.BlockSpec((1, w), lambda bi, i: (0, 0)),
            pl.BlockSpec((1, w), lambda bi, i: (0, 0)),
            pl.BlockSpec((w, w), lambda bi, i: (0, 0)),
        ],
        out_specs=[tr, tok(0), tr],
        out_shape=[jax.ShapeDtypeStruct((b, w, t), BF16), jax.ShapeDtypeStruct((b, t, w), BF16),
                   jax.ShapeDtypeStruct((b, w, t), BF16)],
        compiler_params=_cparams(("arbitrary", "arbitrary")),
        name="attn_prep",
    )(p, p, p, rope_c, rope_s, jnp.tile(qn_w, reps).reshape(1, w), jnp.tile(kn_w, reps).reshape(1, w), bd)


def _attn_body(lam_ref, qt_ref, k_ref, vt_ref, sw_ref, o_ref, m_scr, l_scr, acc_scr, sa_scr, sb_scr,
               *, tq, tk, kv0, nkv, lam_init):
    qt = qt_ref[...]
    row = lax.broadcasted_iota(jnp.int32, qt.shape, 0)
    zero = jnp.zeros_like(qt)
    qm = (jnp.where(row < A_HEAD_DIM, qt, zero), jnp.where(row >= A_HEAD_DIM, qt, zero))
    m_scr[...] = jnp.full(m_scr.shape, -jnp.inf, F32)
    l_scr[...] = jnp.zeros(l_scr.shape, F32)
    acc_scr[...] = jnp.zeros(acc_scr.shape, F32)

    def scores(c, dst):
        off = pl.multiple_of(kv0 + c * tk, tk)
        kc = k_ref[pl.ds(off, tk), :]
        for mi in range(2):
            dst[mi] = jnp.dot(kc, qm[mi], preferred_element_type=F32)

    def accumulate(c, src):
        off = pl.multiple_of(kv0 + c * tk, tk)
        vtc = vt_ref[:, pl.ds(off, tk)]
        s = [src[mi] for mi in range(2)]
        m_old = [m_scr[mi] for mi in range(2)]
        m_new = [jnp.maximum(m_old[mi], jnp.max(s[mi], axis=0, keepdims=True)) for mi in range(2)]
        p = [jnp.exp2(s[mi] - m_new[mi]) for mi in range(2)]
        alpha = [jnp.exp2(m_old[mi] - m_new[mi]) for mi in range(2)]
        for mi in range(2):
            l_scr[mi] = alpha[mi] * l_scr[mi] + jnp.sum(p[mi], axis=0, keepdims=True)
            m_scr[mi] = m_new[mi]
        pv = [jnp.dot(vtc, p[mi].astype(BF16), preferred_element_type=F32) for mi in range(2)]
        for mi in range(2):
            acc_scr[mi] = alpha[mi] * acc_scr[mi] + pv[mi]

    scores(0, sa_scr)
    n_pairs = (nkv - 1) // 2

    def pair(i, carry):
        c = 2 * i
        scores(c + 1, sb_scr)
        accumulate(c, sa_scr)
        scores(c + 2, sa_scr)
        accumulate(c + 1, sb_scr)
        return carry

    lax.fori_loop(0, n_pairs, pair, 0)
    if nkv % 2 == 0:
        scores(nkv - 1, sb_scr)
        accumulate(nkv - 2, sa_scr)
        accumulate(nkv - 1, sb_scr)
    else:
        accumulate(nkv - 1, sa_scr)

    lp = lam_ref[...]
    lam = (jnp.exp(jnp.sum(lp[0:1] * lp[1:2], axis=-1, keepdims=True))
           - jnp.exp(jnp.sum(lp[2:3] * lp[3:4], axis=-1, keepdims=True)) + lam_init)
    o = (acc_scr[0] / l_scr[0] - lam * (acc_scr[1] / l_scr[1])).T
    ms = jnp.mean(o * o, axis=-1, keepdims=True)
    o_ref[...] = (o * lax.rsqrt(ms + SUBLN_EPS) * sw_ref[...] * (1.0 - lam_init)).astype(o_ref.dtype)


def _attention_call(lam_p, qt, k, vt, subln_w, *, q0, nq_rows, tq, kv0, nkv_rows, tk, lam_init, name):
    b, t, _ = k.shape
    hd = A_V_DIM
    assert q0 % tq == 0 and nq_rows % tq == 0 and kv0 % tk == 0 and nkv_rows % tk == 0
    qb0 = q0 // tq
    body = functools.partial(_attn_body, tq=tq, tk=tk, kv0=kv0, nkv=nkv_rows // tk, lam_init=lam_init)
    return pl.pallas_call(
        body,
        grid=(b, A_HEADS, nq_rows // tq),
        in_specs=[
            pl.BlockSpec((4, A_HEAD_DIM), lambda bi, h, i: (0, 0)),
            pl.BlockSpec((None, hd, tq), lambda bi, h, i: (bi, h, qb0 + i)),
            pl.BlockSpec((None, t, hd), lambda bi, h, i: (bi, 0, h)),
            pl.BlockSpec((None, hd, t), lambda bi, h, i: (bi, h, 0)),
            pl.BlockSpec((1, hd), lambda bi, h, i: (0, 0)),
        ],
        out_specs=pl.BlockSpec((None, tq, hd), lambda bi, h, i: (bi, i, h)),
        out_shape=jax.ShapeDtypeStruct((b, nq_rows, A_WIDTH), BF16),
        scratch_shapes=[pltpu.VMEM((2, 1, tq), F32), pltpu.VMEM((2, 1, tq), F32), pltpu.VMEM((2, hd, tq), F32),
                        pltpu.VMEM((2, tk, tq), F32), pltpu.VMEM((2, tk, tq), F32)],
        compiler_params=_cparams(("arbitrary", "arbitrary", "arbitrary")),
        name=name,
    )(lam_p, qt, k, vt, subln_w.reshape(1, hd))


def _diff_attention(lam_p, qt, k, vt, subln_w, s_len, lam_init):
    b, t, _ = k.shape
    ctx_len = t - s_len
    tq = _pick(s_len, (512, 256))
    tk = _pick(t, (1408, 768, 256))
    ya_lat = _attention_call(lam_p, qt, k, vt, subln_w, q0=0, nq_rows=s_len, tq=tq,
                             kv0=0, nkv_rows=t, tk=tk, lam_init=lam_init, name="diff_attention_lat")
    tc = _pick(ctx_len, (256, 128))
    assert s_len % tc == 0
    ya_ctx = _attention_call(lam_p, qt, k, vt, subln_w, q0=s_len, nq_rows=ctx_len, tq=tc,
                             kv0=s_len, nkv_rows=ctx_len, tk=tc, lam_init=lam_init, name="diff_attention_ctx")
    return ya_lat, ya_ctx


def _halo_specs(tm, t, width, col_blk):
    per = tm // HALO
    last = t // HALO - 1
    main = pl.BlockSpec((None, tm, width), lambda bi, i: (bi, i, col_blk))
    prev = pl.BlockSpec((None, HALO, width), lambda bi, i: (bi, jnp.maximum(i * per - 1, 0), col_blk))
    nxt = pl.BlockSpec((None, HALO, width), lambda bi, i: (bi, jnp.minimum((i + 1) * per, last), col_blk))
    return main, prev, nxt


def _halo_ok(i, s_tiles, n_tiles):
    prev_ok = jnp.logical_and(i != 0, i != s_tiles)
    next_ok = jnp.logical_and(i != s_tiles - 1, i != n_tiles - 1)
    return prev_ok, next_ok


def _conv_rows(buf, w_ref, r0, taps, c0, c1):
    base = HALO - taps // 2 + r0
    acc = jnp.zeros((ROW_CHUNK, c1 - c0), F32)
    for j in range(taps):
        acc = acc + buf[base + j:base + j + ROW_CHUNK, c0:c1] * w_ref[j:j + 1, c0:c1]
    return acc


SUBLANES = 8


def _conformer_body(a_ref, ap_ref, an_ref, g_ref, gp_ref, gn_ref, w_ref, b_ref, lw_ref, lb_ref, o_ref, buf, phase,
                    *, tm, s_tiles, n_tiles):
    i = pl.program_id(1)
    prev_ok, next_ok = _halo_ok(i, s_tiles, n_tiles)

    def glu(a, g):
        return a[...].astype(F32) * jax.nn.sigmoid(g[...].astype(F32))

    buf[0:HALO, :] = jnp.where(prev_ok, glu(ap_ref, gp_ref), 0.0)
    buf[HALO:HALO + tm, :] = glu(a_ref, g_ref)
    buf[HALO + tm:2 * HALO + tm, :] = jnp.where(next_ok, glu(an_ref, gn_ref), 0.0)
    n_rows = tm + 2 * HALO - SUBLANES
    for p in range(SUBLANES):
        phase[p, 0:n_rows, :] = buf[p:p + n_rows, :]
    base = HALO - CONV_K // 2
    for r0 in range(0, tm, ROW_CHUNK):
        acc = jnp.zeros((ROW_CHUNK, CONV_CH), F32)
        for j in range(CONV_K):
            o = base + j
            lo = r0 + o - o % SUBLANES
            acc = acc + phase[o % SUBLANES, lo:lo + ROW_CHUNK, :] * w_ref[j:j + 1, :]
        y = acc + b_ref[...]
        mu = jnp.mean(y, axis=-1, keepdims=True)
        yc = y - mu
        var = jnp.mean(yc * yc, axis=-1, keepdims=True)
        z = yc * lax.rsqrt(var + 1e-5) * lw_ref[...] + lb_ref[...]
        o_ref[r0:r0 + ROW_CHUNK, :] = _silu(z).astype(o_ref.dtype)


def _conformer(p, dw_w, dw_b, ln_w, ln_b, s_len):
    b, t, _ = p.shape
    tm = CONV_TM
    ch = CONV_CH
    a_specs = _halo_specs(tm, t, ch, PB_BA)
    g_specs = _halo_specs(tm, t, ch, PB_BG)
    vec = pl.BlockSpec((1, ch), lambda bi, i: (0, 0))
    body = functools.partial(_conformer_body, tm=tm, s_tiles=s_len // tm, n_tiles=t // tm)
    return pl.pallas_call(
        body,
        grid=(b, t // tm),
        in_specs=[*a_specs, *g_specs, pl.BlockSpec((CONV_K, ch), lambda bi, i: (0, 0)), vec, vec, vec],
        out_specs=pl.BlockSpec((None, tm, ch), lambda bi, i: (bi, i, 0)),
        out_shape=jax.ShapeDtypeStruct((b, t, ch), BF16),
        scratch_shapes=[pltpu.VMEM((tm + 2 * HALO, ch), F32), pltpu.VMEM((SUBLANES, tm + 2 * HALO - SUBLANES, ch), F32)],
        compiler_params=_cparams(("arbitrary", "arbitrary")),
        name="conformer_conv",
    )(p, p, p, p, p, p, dw_w, dw_b.reshape(1, ch), ln_w.reshape(1, ch), ln_b.reshape(1, ch))


def _softplus(x):
    return jnp.maximum(x, 0.0) + jnp.log(1.0 + jnp.exp(-jnp.abs(x)))


def _gdnprep_body(x_ref, xp_ref, xn_ref, small_ref, w_ref, alog_ref, dtb_ref, q_ref, k_ref, v_ref, gb_ref, buf,
                  *, tm, s_tiles, n_tiles):
    i = pl.program_id(1)
    prev_ok, next_ok = _halo_ok(i, s_tiles, n_tiles)
    buf[0:HALO, :] = jnp.where(prev_ok, xp_ref[...].astype(F32), 0.0)
    buf[HALO:HALO + tm, :] = x_ref[...].astype(F32)
    buf[HALO + tm:2 * HALO + tm, :] = jnp.where(next_ok, xn_ref[...].astype(F32), 0.0)
    outs = (q_ref, k_ref, v_ref)
    for r0 in range(0, tm, ROW_CHUNK):
        for part in range(3):
            y = _silu(_conv_rows(buf, w_ref, r0, SHORT_K, part * C_WIDTH, (part + 1) * C_WIDTH))
            if part < 2:
                post = C_HEAD_DIM ** -0.5 if part == 0 else 1.0
                for h in range(C_HEADS):
                    yh = y[:, h * C_HEAD_DIM:(h + 1) * C_HEAD_DIM]
                    n = yh * lax.rsqrt(jnp.sum(yh * yh, axis=-1, keepdims=True) + 1e-6)
                    outs[part][r0:r0 + ROW_CHUNK, h * C_HEAD_DIM:(h + 1) * C_HEAD_DIM] = n * post
            else:
                outs[part][r0:r0 + ROW_CHUNK, :] = y
    sm = small_ref[...]
    lane = lax.broadcasted_iota(jnp.int32, sm.shape, 1)
    gdec = -jnp.exp(alog_ref[...]) * _softplus(sm + dtb_ref[...])
    gb_ref[...] = jnp.where(lane < 2 * C_HEADS, jax.nn.sigmoid(sm), gdec)


def _gdn_prep(p, small, conv_w, a_log, dt_bias, s_len):
    b, t, _ = p.shape
    tm = CONV_TM
    w3 = 3 * C_WIDTH
    x_specs = _halo_specs(tm, t, w3, PB_CQKV)
    pad = SMALL_W - 4 * C_HEADS
    alog = jnp.concatenate([jnp.zeros((2 * C_HEADS,), F32), a_log.reshape(-1), jnp.zeros((pad,), F32)]).reshape(1, SMALL_W)
    dtb = jnp.concatenate([jnp.zeros((2 * C_HEADS,), F32), dt_bias.reshape(-1), jnp.zeros((pad,), F32)]).reshape(1, SMALL_W)
    vec = pl.BlockSpec((1, SMALL_W), lambda bi, i: (0, 0))
    out_blk = pl.BlockSpec((None, tm, C_WIDTH), lambda bi, i: (bi, i, 0))
    body = functools.partial(_gdnprep_body, tm=tm, s_tiles=s_len // tm, n_tiles=t // tm)
    return pl.pallas_call(
        body,
        grid=(b, t // tm),
        in_specs=[*x_specs, pl.BlockSpec((None, tm, SMALL_W), lambda bi, i: (bi, i, 0)),
                  pl.BlockSpec((SHORT_K, w3), lambda bi, i: (0, 0)), vec, vec],
        out_specs=[out_blk, out_blk, out_blk, pl.BlockSpec((None, tm, SMALL_W), lambda bi, i: (bi, i, 0))],
        out_shape=[jax.ShapeDtypeStruct((b, t, C_WIDTH), F32)] * 3 + [jax.ShapeDtypeStruct((b, t, SMALL_W), F32)],
        scratch_shapes=[pltpu.VMEM((tm + 2 * HALO, w3), F32)],
        compiler_params=_cparams(("arbitrary", "arbitrary")),
        name="gdn_prep",
    )(p, p, p, small, conv_w, alog, dtb)


def _mm(a, b):
    return jnp.dot(a.astype(BF16), b.astype(BF16), preferred_element_type=F32)


def _mm_nt(a, b):
    return lax.dot_general(a.astype(BF16), b.astype(BF16), (((1,), (1,)), ((), ())), preferred_element_type=F32)


def _mm_tn(a, b):
    return lax.dot_general(a.astype(BF16), b.astype(BF16), (((0,), (0,)), ((), ())), preferred_element_type=F32)


def _gdn_chunks(chains):
    n = len(chains)
    c = chains[0][0].shape[0]
    ii = lax.broadcasted_iota(jnp.int32, (c, c), 0)
    jj = lax.broadcasted_iota(jnp.int32, (c, c), 1)
    eye = (ii == jj).astype(F32)
    incl = [(ii <= jj) if ch[7] else (ii >= jj) for ch in chains]
    strict = [(ii < jj) if ch[7] else (ii > jj) for ch in chains]
    q = [ch[0] for ch in chains]
    k = [ch[1] for ch in chains]
    v = [ch[2] for ch in chains]
    gcol = [ch[3] for ch in chains]
    bcol = [ch[5] for ch in chains]
    state = [ch[6] for ch in chains]
    decay = [jnp.where(incl[i], jnp.exp(jnp.where(incl[i], gcol[i] - chains[i][4], 0.0)), 0.0) for i in range(n)]
    kb = [k[i] * bcol[i] for i in range(n)]
    low = [jnp.where(strict[i], _mm_nt(kb[i], k[i]) * decay[i], 0.0) for i in range(n)]
    intra = [jnp.where(incl[i], _mm_nt(q[i], k[i]) * decay[i], 0.0) for i in range(n)]
    blk = min(c, NEUMANN_BLK)
    assert c in (blk, 2 * blk)
    if c > blk:
        same = (ii // blk) == (jj // blk)
        low_d = [jnp.where(same, low[i], 0.0) for i in range(n)]
        low_o = [jnp.where(same, 0.0, low[i]) for i in range(n)]
    else:
        low_d = low
    pw = [-low_d[i] for i in range(n)]
    t_inv = [eye + pw[i] for i in range(n)]
    for _ in range(int(math.log2(blk)) - 1):
        pw = [_mm(pw[i], pw[i]) for i in range(n)]
        t_inv = [t_inv[i] + _mm(t_inv[i], pw[i]) for i in range(n)]
    if c > blk:
        od = [_mm(low_o[i], t_inv[i]) for i in range(n)]
        t_inv = [t_inv[i] - _mm(t_inv[i], od[i]) for i in range(n)]
    eg = [jnp.exp(gcol[i]) for i in range(n)]
    u = [_mm(t_inv[i], v[i] * bcol[i]) for i in range(n)]
    w = [_mm(t_inv[i], kb[i] * eg[i]) for i in range(n)]
    g_last = [gcol[i][0:1, :] if chains[i][7] else gcol[i][c - 1:c, :] for i in range(n)]
    k_dec = [k[i] * jnp.exp(g_last[i] - gcol[i]) for i in range(n)]
    q_dec = [q[i] * eg[i] for i in range(n)]
    v_new = [u[i] - _mm(w[i], state[i]) for i in range(n)]
    o = [_mm(q_dec[i], state[i]) + _mm(intra[i], v_new[i]) for i in range(n)]
    new_state = [state[i] * jnp.exp(g_last[i]) + _mm_tn(k_dec[i], v_new[i]) for i in range(n)]
    return list(zip(o, new_state))


def _gdn_body(qf_ref, kf_ref, vf_ref, gf_ref, qb_ref, kb_ref, vb_ref, gbb_ref, of_ref, ob_ref, st_scr):
    @pl.when(pl.program_id(1) == 0)
    def _():
        st_scr[...] = jnp.zeros(st_scr.shape, F32)

    c = GDN_CHUNK
    ii = lax.broadcasted_iota(jnp.int32, (c, c), 0)
    jj = lax.broadcasted_iota(jnp.int32, (c, c), 1)
    dirs = ((qf_ref, kf_ref, vf_ref, gf_ref, of_ref, False), (qb_ref, kb_ref, vb_ref, gbb_ref, ob_ref, True))
    chains = []
    for d, (q_ref, k_ref, v_ref, g_ref, o_ref, rev) in enumerate(dirs):
        gb = g_ref[...]
        tri = ((ii <= jj) if rev else (ii >= jj)).astype(F32)
        gc = jnp.dot(tri, gb, preferred_element_type=F32, precision=HIGHEST)
        gct = gc.T
        for h in range(C_HEADS):
            sl = slice(h * C_HEAD_DIM, (h + 1) * C_HEAD_DIM)
            bc = d * C_HEADS + h
            gcn = 2 * C_HEADS + bc
            chains.append((q_ref[:, sl], k_ref[:, sl], v_ref[:, sl], gc[:, gcn:gcn + 1], gct[gcn:gcn + 1, :],
                           gb[:, bc:bc + 1], st_scr[bc], rev))
    results = _gdn_chunks(chains)
    for d, (_, _, _, _, o_ref, _) in enumerate(dirs):
        for h in range(C_HEADS):
            bc = d * C_HEADS + h
            o, st = results[bc]
            o_ref[:, h * C_HEAD_DIM:(h + 1) * C_HEAD_DIM] = o
            st_scr[bc] = st


def _gdn_scan(q, k, v, gb, s_len):
    b, t, w = q.shape
    c = GDN_CHUNK
    n_lat = s_len // c
    n_all = t // c
    n_ctx = n_all - n_lat

    def fwd(bi, s):
        return (bi, jnp.where(s < n_ctx, n_lat + s, s - n_ctx), 0)

    def bwd(bi, s):
        return (bi, jnp.where(s < n_ctx, n_all - 1 - s, n_all - 1 - s), 0)

    wide_f = pl.BlockSpec((None, c, w), fwd)
    wide_b = pl.BlockSpec((None, c, w), bwd)
    nar_f = pl.BlockSpec((None, c, SMALL_W), fwd)
    nar_b = pl.BlockSpec((None, c, SMALL_W), bwd)
    return pl.pallas_call(
        _gdn_body,
        grid=(b, n_all),
        in_specs=[wide_f, wide_f, wide_f, nar_f, wide_b, wide_b, wide_b, nar_b],
        out_specs=[wide_f, wide_b],
        out_shape=[jax.ShapeDtypeStruct((b, t, w), F32)] * 2,
        scratch_shapes=[pltpu.VMEM((2 * C_HEADS, C_HEAD_DIM, C_HEAD_DIM), F32)],
        compiler_params=_cparams(("arbitrary", "arbitrary")),
        name="gdn_scan",
    )(q, k, v, gb, q, k, v, gb)


def _merge_body(x_ref, yal_ref, yac_ref, yb_ref, of_ref, ob_ref, z_ref, g0_ref, g1_ref, g2_ref, gnw_ref, wb_ref,
                wo_ref, gate_ref, n2_ref, sh_ref, sc_ref, rw_ref, xo_ref, h2_ref, sco_ref, *, tm, s_len):
    i = pl.program_id(1)
    is_ctx = _rows_are_ctx(i, tm, s_len)
    ya = jnp.where(i * tm >= s_len, yac_ref[...], yal_ref[...])
    o = of_ref[...] + ob_ref[...]
    z = z_ref[...].astype(F32)
    parts = []
    for h in range(C_HEADS):
        sl = slice(h * C_HEAD_DIM, (h + 1) * C_HEAD_DIM)
        oh = o[:, sl]
        ms = jnp.mean(oh * oh, axis=-1, keepdims=True)
        parts.append((oh * lax.rsqrt(ms + EPS) * gnw_ref[...] * _silu(z[:, sl])).astype(BF16))
    yc = jnp.concatenate(parts, axis=-1)
    m = jnp.zeros((tm, D_MODEL), F32)
    for bi, (y, g_ref) in enumerate(((ya, g0_ref), (yb_ref[...], g1_ref), (yc, g2_ref))):
        m = m + jax.nn.sigmoid(g_ref[...].astype(F32)) * jnp.dot(y, wb_ref[bi], preferred_element_type=F32)
    upd = jnp.dot(m.astype(BF16), wo_ref[...], preferred_element_type=F32)
    x = x_ref[...] + _pick_mod(gate_ref, is_ctx) * upd
    xo_ref[...] = x
    h2 = _rms_modulate(x, n2_ref[...], _pick_mod(sh_ref, is_ctx), _pick_mod(sc_ref, is_ctx))
    h2_ref[...] = h2
    sco_ref[...] = jax.nn.sigmoid(jnp.dot(h2, rw_ref[...], preferred_element_type=F32, precision=HIGHEST))


def _merge(xs, ya_lat, ya_ctx, yb, o_f, o_b, p, gdn_norm_w, w_branch, w_out, modb, norm2_w, router_w, s_len):
    b, t, d = xs.shape
    tm = CONV_TM
    bw = BRANCH_W
    e_pad = LANES
    s_tiles = s_len // tm
    n_tiles = t // tm
    lat_spec = pl.BlockSpec((None, tm, bw), lambda bi, i: (bi, jnp.minimum(i, s_tiles - 1), 0))
    ctx_spec = pl.BlockSpec((None, tm, bw), lambda bi, i: (bi, jnp.maximum(i - s_tiles, 0), 0))
    rw = jnp.zeros((d, e_pad), F32).at[:, :N_EXPERTS].set(router_w)
    tok = lambda width, blk: pl.BlockSpec((None, tm, width), lambda bi, i: (bi, i, blk))
    modspec = lambda k: pl.BlockSpec((None, 2, d), lambda bi, i: (bi, 0, k))
    full = lambda shape: pl.BlockSpec(shape, lambda bi, i: (0,) * len(shape))
    body = functools.partial(_merge_body, tm=tm, s_len=s_len)
    return pl.pallas_call(
        body,
        grid=(b, t // tm),
        in_specs=[
            tok(d, 0), lat_spec, ctx_spec, tok(bw, 0), tok(bw, 0), tok(bw, 0), tok(bw, PB_CZ),
            tok(d, 0), tok(d, 1), tok(d, 2),
            full((1, C_HEAD_DIM)), full((N_BRANCH, bw, d)), full((d, d)),
            modspec(2), full((1, d)), modspec(3), modspec(4), full((d, e_pad)),
        ],
        out_specs=[tok(d, 0), tok(d, 0), tok(e_pad, 0)],
        out_shape=[jax.ShapeDtypeStruct((b, t, d), F32), jax.ShapeDtypeStruct((b, t, d), F32),
                   jax.ShapeDtypeStruct((b, t, e_pad), F32)],
        compiler_params=_cparams(("arbitrary", "arbitrary")),
        name="merge_norm_router",
    )(xs, ya_lat, ya_ctx, yb, o_f, o_b, p, p, p, p, gdn_norm_w.reshape(1, C_HEAD_DIM), w_branch, w_out,
      modb, norm2_w.reshape(1, d), modb, modb, rw)


MOE_ROWS = 256


def _routing_tables(scores, router_bias):
    n = scores.shape[0]
    rb = MOE_ROWS
    grp = (scores + router_bias.astype(F32)).reshape(n, N_GROUPS, EXPERTS_PER_GROUP)

    def top2(a):
        ids = lax.broadcasted_iota(jnp.int32, a.shape, a.ndim - 1)
        i1 = jnp.argmax(a, axis=-1)
        v1 = jnp.max(a, axis=-1)
        rest = jnp.where(ids == i1[..., None], -jnp.inf, a)
        i2 = jnp.argmax(rest, axis=-1)
        v2 = jnp.max(rest, axis=-1)
        return (v1, v2), jnp.stack([i1, i2], axis=-1).astype(jnp.int32)

    (g1, g2), _ = top2(grp)
    best = jnp.argmax(g1 + g2, axis=-1)
    in_grp = jnp.take_along_axis(grp, best[:, None, None], axis=1)[:, 0]
    _, local = top2(in_grp)
    expert = best[:, None] * EXPERTS_PER_GROUP + local
    w = jnp.take_along_axis(scores, expert, axis=-1)
    gate = w / jnp.sum(w, axis=-1, keepdims=True)
    lo = jnp.min(local, axis=-1)
    hi = jnp.max(local, axis=-1)
    first_is_lo = local[:, 0] < local[:, 1]
    g_lo = jnp.where(first_is_lo, gate[:, 0], gate[:, 1])
    g_hi = jnp.where(first_is_lo, gate[:, 1], gate[:, 0])
    pair = lo * (2 * EXPERTS_PER_GROUP - lo - 1) // 2 + (hi - lo - 1)
    cls = (best * N_PAIR + pair).astype(jnp.int32)

    blk = LANES
    onehot = (cls[:, None] == jnp.arange(N_CLASS, dtype=jnp.int32)[None, :]).astype(F32).reshape(n // blk, blk, N_CLASS)
    tril = (jnp.arange(blk)[:, None] >= jnp.arange(blk)[None, :]).astype(F32)
    within = jnp.einsum("ij,bjc->bic", tril, onehot)
    blk_tot = within[:, -1, :]
    nb = n // blk
    before = (jnp.arange(nb)[:, None] > jnp.arange(nb)[None, :]).astype(F32)
    blk_off = jnp.dot(before, blk_tot, precision=HIGHEST)
    counts = (blk_off[-1] + blk_tot[-1]).astype(jnp.int32)
    rank = jnp.sum(onehot * (within - 1.0 + blk_off[:, None, :]), axis=-1).reshape(n).astype(jnp.int32)
    padded = (counts + rb - 1) // rb * rb
    upto = jnp.arange(N_CLASS)[:, None] >= jnp.arange(N_CLASS)[None, :]
    pad_end = jnp.sum(jnp.where(upto, padded[None, :], 0), axis=1)
    pad_start = (pad_end - padded).astype(F32)
    dest = (jnp.sum(onehot * pad_start[None, None, :], axis=-1).reshape(n).astype(jnp.int32) + rank)
    n_blocks = n // rb + N_CLASS
    n_rows = n_blocks * rb
    row_tok = jnp.zeros((n_rows,), jnp.int32).at[dest].set(jnp.arange(n, dtype=jnp.int32))
    tok_row = dest
    gate_tok = jnp.concatenate([g_lo[:, None], g_hi[:, None], jnp.zeros((n, LANES - 2), F32)], axis=1)
    n_used = (pad_end[-1] // rb).astype(jnp.int32)
    blk = jnp.minimum(jnp.arange(n_blocks, dtype=jnp.int32), n_used - 1)
    blk_cls = jnp.minimum(jnp.searchsorted(pad_end, blk * rb, side="right"), N_CLASS - 1).astype(jnp.int32)
    pl_lo, pl_hi = [], []
    for a in range(EXPERTS_PER_GROUP):
        for c in range(a + 1, EXPERTS_PER_GROUP):
            pl_lo.append(a)
            pl_hi.append(c)
    pl_lo = jnp.asarray(pl_lo, jnp.int32)
    pl_hi = jnp.asarray(pl_hi, jnp.int32)
    blk_grp = blk_cls // N_PAIR
    blk_pair = blk_cls % N_PAIR
    e_lo = blk_grp * EXPERTS_PER_GROUP + pl_lo[blk_pair]
    e_hi = blk_grp * EXPERTS_PER_GROUP + pl_hi[blk_pair]
    return e_lo, e_hi, n_used.reshape(1), row_tok, tok_row, gate_tok


def _row_copy(idx_ref, pos, r, src_hbm, dst, sem):
    return pltpu.make_async_copy(src_hbm.at[pl.ds(idx_ref[pos], 1), :], dst.at[pl.ds(r, 1), :], sem)


def _start_row_gather(idx_ref, base, n_rows, src_hbm, dst, sem):
    for r in range(n_rows):
        _row_copy(idx_ref, base + r, r, src_hbm, dst, sem).start()


def _wait_row_gather(n_rows, src_hbm, dst, sem):
    pltpu.make_async_copy(src_hbm.at[pl.ds(0, n_rows), :], dst, sem).wait()


def _moe_body(elo_ref, ehi_ref, nused_ref, rowtok_ref, h2_hbm, wg_lo, wu_lo, wd_lo, wg_hi, wu_hi, wd_hi,
              y_ref, xbuf, sem, *, rb, n_blocks):
    i = pl.program_id(0)
    n_used = nused_ref[0]
    slot = i % 2

    @pl.when(i == 0)
    def _():
        _start_row_gather(rowtok_ref, 0, rb, h2_hbm, xbuf.at[0], sem.at[0])

    @pl.when(i <= n_used)
    def _():
        _wait_row_gather(rb, h2_hbm, xbuf.at[slot], sem.at[slot])

    @pl.when(i < n_used)
    def _():
        x = xbuf[slot].astype(BF16)
        d = x.shape[-1]
        nxt = jnp.minimum(i + 1, n_blocks - 1)
        _start_row_gather(rowtok_ref, nxt * rb, rb, h2_hbm, xbuf.at[1 - slot], sem.at[1 - slot])

        def ffn(wg, wu, wd):
            hg = jnp.dot(x, wg[...], preferred_element_type=F32)
            hu = jnp.dot(x, wu[...], preferred_element_type=F32)
            return jnp.dot((_silu(hg) * hu).astype(BF16), wd[...], preferred_element_type=F32)

        y_ref[:, :d] = ffn(wg_lo, wu_lo, wd_lo)
        y_ref[:, d:] = ffn(wg_hi, wu_hi, wd_hi)

    @pl.when(i >= n_used)
    def _():
        y_ref[...] = jnp.zeros(y_ref.shape, F32)

    @pl.when(jnp.logical_and(i == n_blocks - 1, i < n_used))
    def _():
        _wait_row_gather(rb, h2_hbm, xbuf.at[1 - slot], sem.at[1 - slot])


def _moe_ffn(h2_flat, tables, w_gate, w_up, w_down):
    e_lo, e_hi, n_used, row_tok, _, _ = tables
    n, d = h2_flat.shape
    rb = MOE_ROWS
    n_blocks = e_lo.shape[0]
    de = w_gate.shape[-1]
    w_in_lo = pl.BlockSpec((None, d, de), lambda i, lo, hi, nu, rt: (lo[i], 0, 0))
    w_in_hi = pl.BlockSpec((None, d, de), lambda i, lo, hi, nu, rt: (hi[i], 0, 0))
    w_dn_lo = pl.BlockSpec((None, de, d), lambda i, lo, hi, nu, rt: (lo[i], 0, 0))
    w_dn_hi = pl.BlockSpec((None, de, d), lambda i, lo, hi, nu, rt: (hi[i], 0, 0))
    grid_spec = pltpu.PrefetchScalarGridSpec(
        num_scalar_prefetch=4,
        grid=(n_blocks,),
        in_specs=[
            pl.BlockSpec(memory_space=pl.ANY),
            w_in_lo, w_in_lo, w_dn_lo, w_in_hi, w_in_hi, w_dn_hi,
        ],
        out_specs=pl.BlockSpec((rb, 2 * d), lambda i, lo, hi, nu, rt: (i, 0)),
        scratch_shapes=[pltpu.VMEM((2, rb, d), F32), pltpu.SemaphoreType.DMA((2,))],
    )
    return pl.pallas_call(
        functools.partial(_moe_body, rb=rb, n_blocks=n_blocks),
        grid_spec=grid_spec,
        out_shape=jax.ShapeDtypeStruct((n_blocks * rb, 2 * d), F32),
        compiler_params=_cparams(("arbitrary",)),
        name="moe_expert_ffn",
    )(e_lo, e_hi, n_used, row_tok, h2_flat, w_gate, w_up, w_down, w_gate, w_up, w_down)


def _combine_body(tokrow_ref, x_ref, gate_ref, eg_ref, y_hbm, o_ref, ybuf, sem, *, tm, s_len, tiles_per_batch, n_steps):
    bi = pl.program_id(0)
    i = pl.program_id(1)
    step = bi * tiles_per_batch + i
    slot = step % 2

    @pl.when(step == 0)
    def _():
        _start_row_gather(tokrow_ref, 0, tm, y_hbm, ybuf.at[0], sem.at[0])

    _wait_row_gather(tm, y_hbm, ybuf.at[slot], sem.at[slot])

    @pl.when(step + 1 < n_steps)
    def _():
        _start_row_gather(tokrow_ref, (step + 1) * tm, tm, y_hbm, ybuf.at[1 - slot], sem.at[1 - slot])

    is_ctx = _rows_are_ctx(i, tm, s_len)
    d = x_ref.shape[-1]
    eg = eg_ref[...]
    y = ybuf[slot]
    f = eg[:, 0:1] * y[:, :d] + eg[:, 1:2] * y[:, d:]
    o_ref[...] = x_ref[...] + _pick_mod(gate_ref, is_ctx) * f


def _combine(xs, y_sorted, tok_row, gate_tok, modb, s_len):
    b, t, d = xs.shape
    tm = CONV_TM
    grid_spec = pltpu.PrefetchScalarGridSpec(
        num_scalar_prefetch=1,
        grid=(b, t // tm),
        in_specs=[
            pl.BlockSpec((None, tm, d), lambda bi, i, tr: (bi, i, 0)),
            pl.BlockSpec((None, 2, d), lambda bi, i, tr: (bi, 0, 5)),
            pl.BlockSpec((None, tm, LANES), lambda bi, i, tr: (bi, i, 0)),
            pl.BlockSpec(memory_space=pl.ANY),
        ],
        out_specs=pl.BlockSpec((None, tm, d), lambda bi, i, tr: (bi, i, 0)),
        scratch_shapes=[pltpu.VMEM((2, tm, 2 * d), F32), pltpu.SemaphoreType.DMA((2,))],
    )
    body = functools.partial(_combine_body, tm=tm, s_len=s_len, tiles_per_batch=t // tm, n_steps=b * (t // tm))
    return pl.pallas_call(
        body,
        grid_spec=grid_spec,
        out_shape=jax.ShapeDtypeStruct((b, t, d), F32),
        compiler_params=_cparams(("arbitrary", "arbitrary")),
        name="moe_combine_residual",
    )(tok_row, xs, modb, gate_tok.reshape(b, t, LANES), y_sorted)


def _rope_tables(s_len, ctx_len):
    t = jnp.arange(s_len, dtype=jnp.int32)
    r = (t // GRID_W).astype(F32)
    col = (t % GRID_W).astype(F32)
    axis_dim = A_HEAD_DIM // 2
    inv_freq = ROPE_THETA ** (-jnp.arange(0, axis_dim, 2, dtype=F32) / axis_dim)
    ang_r = r[:, None] * inv_freq
    ang_c = col[:, None] * inv_freq
    cos = jnp.concatenate([jnp.cos(ang_r), jnp.cos(ang_r), jnp.cos(ang_c), jnp.cos(ang_c)], axis=-1)
    sin = jnp.concatenate([-jnp.sin(ang_r), jnp.sin(ang_r), -jnp.sin(ang_c), jnp.sin(ang_c)], axis=-1)
    reps = A_WIDTH // A_HEAD_DIM
    cos = jnp.concatenate([jnp.tile(cos, (1, reps)), jnp.ones((ctx_len, A_WIDTH), F32)], axis=0)
    sin = jnp.concatenate([jnp.tile(sin, (1, reps)), jnp.zeros((ctx_len, A_WIDTH), F32)], axis=0)
    return cos, sin


def _split_w_in(w_in_l):
    oa, oc, ob = A_COLS, A_COLS + C_COLS, A_COLS + C_COLS + B_COLS
    a = w_in_l[:, :oa]
    c_main = w_in_l[:, oa:oa + 4 * C_WIDTH]
    c_small = w_in_l[:, oa + 4 * C_WIDTH:oc]
    bb = w_in_l[:, oc:ob]
    gates = w_in_l[:, ob:]
    main = jnp.concatenate([gates, a, c_main, bb], axis=1).astype(BF16)
    small = jnp.zeros((w_in_l.shape[0], SMALL_W), F32).at[:, :4 * C_HEADS].set(c_small).astype(BF16)
    return main, small


def kernel(x, c, ctx, c_ctx, w_mod, b_mod, norm1_w, norm2_w, w_in, qn_w, kn_w, lam_p, subln_w, conv_dw_w, conv_dw_b,
           conv_ln_w, conv_ln_b, gdn_conv_w, gdn_a_log, gdn_dt_bias, gdn_norm_w, w_branch, w_out, router_w,
           router_bias, w_gate, w_up, w_down):
    b, s_len, d = x.shape
    ctx_len = ctx.shape[1]
    depth = w_mod.shape[0]
    t = s_len + ctx_len
    assert s_len % CONV_TM == 0 and ctx_len % CONV_TM == 0 and s_len % GRID_W == 0

    xs = jnp.concatenate([x, ctx], axis=1)
    rows = -(-(b + 1) // 8) * 8
    cvec = jnp.zeros((rows, d), F32).at[:b].set(c).at[b].set(c_ctx)
    mod = _modulation(cvec, w_mod, b_mod)
    rope_c, rope_s = _rope_tables(s_len, ctx_len)

    for l in range(depth):
        lam_init = 0.8 - 0.6 * math.exp(-0.3 * l)
        modb = jnp.stack([mod[l, :b], jnp.broadcast_to(mod[l, b], (b, N_MOD * d))], axis=1)
        w_main, w_small = _split_w_in(w_in[l])

        p, small = _in_projection(xs, norm1_w[l], modb, w_main, w_small, s_len)
        qt, k, vt = _attn_prep(p, rope_c, rope_s, qn_w[l], kn_w[l])
        ya_lat, ya_ctx = _diff_attention(lam_p[l], qt, k, vt, subln_w[l], s_len, lam_init)
        yb = _conformer(p, conv_dw_w[l], conv_dw_b[l], conv_ln_w[l], conv_ln_b[l], s_len)
        gq, gk, gv, gb = _gdn_prep(p, small, gdn_conv_w[l], gdn_a_log[l], gdn_dt_bias[l], s_len)
        o_f, o_b = _gdn_scan(gq, gk, gv, gb, s_len)
        xs, h2, scores = _merge(xs, ya_lat, ya_ctx, yb, o_f, o_b, p, gdn_norm_w[l], w_branch[l].astype(BF16),
                                w_out[l].astype(BF16), modb, norm2_w[l], router_w, s_len)

        tables = _routing_tables(scores.reshape(b * t, -1)[:, :N_EXPERTS], router_bias)
        y_sorted = _moe_ffn(h2.reshape(b * t, d), tables, w_gate[l].astype(BF16), w_up[l].astype(BF16),
                            w_down[l].astype(BF16))
        xs = _combine(xs, y_sorted, tables[4], tables[5], modb, s_len)
    return xs[:, :s_len]
```
